```python
import jax, jax.numpy as jnp
from jax import lax
import numpy as np

D_MODEL = 1024
BATCH = 4
SEQ = 4096
DEPTH = 1

SB_HEADS = 8
SB_HEAD_DIM = 64
SB_WIDTH = SB_HEADS * SB_HEAD_DIM
SB_QBLOCK = 128
DN_HEADS = 8
DN_KEY_DIM = 64
DN_VAL_DIM = 128
DN_QK_WIDTH = DN_HEADS * DN_KEY_DIM
DN_V_WIDTH = DN_HEADS * DN_VAL_DIM
DN_CONV_CH = 2 * DN_QK_WIDTH + DN_V_WIDTH
DN_CONV_WIDTH = 4
DN_CHUNK = 64
D_FF = 2816
FFN_CONV_WIDTH = 3
NORM_EPS = 1e-6
L2_EPS = 1e-6
IN_SIZES = (3 * SB_WIDTH, DN_CONV_CH, DN_V_WIDTH, DN_HEADS, DN_HEADS, 2 * D_MODEL)
IN_WIDTH = int(sum(IN_SIZES))
IN_SPLITS = tuple(int(s) for s in np.cumsum(IN_SIZES)[:-1])

kernel_name = "hybrid_stickbreak_gdn_convffn_block"


def rmsnorm(x, w):
    xf = x.astype(jnp.float32)
    y = xf * lax.rsqrt(jnp.mean(xf * xf, axis=-1, keepdims=True) + NORM_EPS)
    return (y * w.astype(jnp.float32)).astype(x.dtype)


def modulate(h, shift, scale):
    return h * (1.0 + scale) + shift


def causal_dwconv(x, w):
    K, C = w.shape
    return lax.conv_general_dilated(
        x, w[:, None, :], window_strides=(1,), padding=[(K - 1, 0)],
        dimension_numbers=("NWC", "WIO", "NWC"), feature_group_count=C)


def stick_breaking_attention(q, k, v):
    S = q.shape[1]
    scale = q.shape[-1] ** -0.5
    outs = []
    for blk in range(S // SB_QBLOCK):
        s0 = blk * SB_QBLOCK
        end = s0 + SB_QBLOCK
        z = jnp.einsum("bthd,bshd->bhts", q[:, s0:end], k[:, :end]).astype(jnp.float32) * scale
        t_pos = s0 + jnp.arange(SB_QBLOCK)
        s_pos = jnp.arange(end)
        causal = s_pos[None, :] < t_pos[:, None]
        log_1m_beta = jnp.where(causal, jax.nn.log_sigmoid(-z), 0.0)
        log_stick = lax.cumsum(log_1m_beta, axis=3, reverse=True) - log_1m_beta
        A = jnp.where(causal, jnp.exp(jax.nn.log_sigmoid(z) + log_stick), 0.0)
        outs.append(jnp.einsum("bhts,bshd->bthd", A.astype(v.dtype), v[:, :end]))
    return jnp.concatenate(outs, axis=1)


def _l2norm(t):
    return t * lax.rsqrt(jnp.sum(t * t, axis=-1, keepdims=True) + L2_EPS)


def gated_delta_rule(q, k, v, g, beta):
    out_dtype = v.dtype
    f32 = jnp.float32
    b, s, h, dk = q.shape
    dv = v.shape[-1]
    C = DN_CHUNK
    n = s // C
    q = _l2norm(q.astype(f32)) * (dk ** -0.5)
    k = _l2norm(k.astype(f32))
    v = v.astype(f32)

    def to_chunks(t):
        return t.reshape(b, n, C, h, t.shape[-1]).transpose(0, 3, 1, 2, 4)

    qc, kc, vc = to_chunks(q), to_chunks(k), to_chunks(v)
    gcum = jnp.cumsum(g.astype(f32).reshape(b, n, C, h).transpose(0, 3, 1, 2), axis=-1)
    bc = beta.astype(f32).reshape(b, n, C, h).transpose(0, 3, 1, 2)[..., None]
    incl = jnp.tril(jnp.ones((C, C), bool))
    strict = jnp.tril(jnp.ones((C, C), bool), -1)
    diff = gcum[..., :, None] - gcum[..., None, :]
    decay = jnp.where(incl, jnp.exp(jnp.where(incl, diff, 0.0)), 0.0)
    kb = kc * bc
    L = jnp.where(strict, jnp.einsum("bhnid,bhnjd->bhnij", kb, kc) * decay, 0.0)
    eye = jnp.eye(C, dtype=f32)
    T = lax.linalg.triangular_solve(eye + L, jnp.broadcast_to(eye, L.shape),
                                    left_side=True, lower=True, unit_diagonal=True)
    u = jnp.einsum("bhnij,bhnjd->bhnid", T, vc * bc)
    w = jnp.einsum("bhnij,bhnjd->bhnid", T, kb * jnp.exp(gcum)[..., None])
    intra = jnp.where(incl, jnp.einsum("bhnid,bhnjd->bhnij", qc, kc) * decay, 0.0)

    def step(state, inp):
        q_i, k_i, u_i, w_i, g_i, a_i = inp
        v_new = u_i - jnp.einsum("bhck,bhkv->bhcv", w_i, state)
        o = (jnp.einsum("bhck,bhkv->bhcv", q_i * jnp.exp(g_i)[..., None], state)
             + jnp.einsum("bhij,bhjv->bhiv", a_i, v_new))
        g_last = g_i[..., -1:]
        k_dec = k_i * jnp.exp(g_last - g_i)[..., None]
        state = state * jnp.exp(g_last)[..., None] + jnp.einsum("bhck,bhcv->bhkv", k_dec, v_new)
        return state, o

    mv = lambda t: jnp.moveaxis(t, 2, 0)
    xs = (mv(qc), mv(kc), mv(u), mv(w), mv(gcum), mv(intra))
    state0 = jnp.zeros((b, h, dk, dv), f32)
    _, o = lax.scan(step, state0, xs)
    o = o.transpose(1, 0, 3, 2, 4).reshape(b, s, h, dv)
    return o.astype(out_dtype)


def setup_inputs(seed: int = 0) -> dict:
    key = jax.random.key(seed)
    ks = jax.random.split(key, 24)
    f32 = jnp.float32
    nrm = lambda k, shape, fan_in: jax.random.normal(k, shape, f32) * (fan_in ** -0.5)
    gain = lambda k, shape: 1.0 + 0.02 * jax.random.normal(k, shape, f32)
    x = jax.random.normal(ks[0], (BATCH, SEQ, D_MODEL), f32)
    c = jax.random.normal(ks[1], (BATCH, D_MODEL), f32)
    w_ada = nrm(ks[2], (DEPTH, D_MODEL, 6 * D_MODEL), D_MODEL)
    b_ada = 0.02 * jax.random.normal(ks[3], (DEPTH, 6 * D_MODEL), f32)
    norm1_w = gain(ks[4], (DEPTH, D_MODEL))
    w_in = nrm(ks[5], (DEPTH, D_MODEL, IN_WIDTH), D_MODEL)
    dn_conv_w = nrm(ks[6], (DEPTH, DN_CONV_WIDTH, DN_CONV_CH), DN_CONV_WIDTH)
    dn_A_log = jnp.log(jax.random.uniform(ks[7], (DEPTH, DN_HEADS), f32, 1.0, 16.0))
    dt = jnp.exp(jax.random.uniform(ks[8], (DEPTH, DN_HEADS), f32, jnp.log(0.001), jnp.log(0.1)))
    dn_dt_bias = dt + jnp.log(-jnp.expm1(-dt))
    dn_norm_w = gain(ks[9], (DEPTH, DN_VAL_DIM))
    w_proj_sb = nrm(ks[10], (DEPTH, SB_WIDTH, D_MODEL), SB_WIDTH)
    w_proj_dn = nrm(ks[11], (DEPTH, DN_V_WIDTH, D_MODEL), DN_V_WIDTH)
    w_out = nrm(ks[12], (DEPTH, D_MODEL, D_MODEL), D_MODEL)
    norm2_w = gain(ks[13], (DEPTH, D_MODEL))
    w_ffn_in = nrm(ks[14], (DEPTH, D_MODEL, 2 * D_FF), D_MODEL)
    ffn_conv_w = nrm(ks[15], (DEPTH, FFN_CONV_WIDTH, 2 * D_FF), FFN_CONV_WIDTH)
    ffn_conv_b = 0.02 * jax.random.normal(ks[16], (DEPTH, 2 * D_FF), f32)
    w_ffn_out = nrm(ks[17], (DEPTH, D_FF, D_MODEL), D_FF)
    final_norm_w = gain(ks[18], (D_MODEL,))
    return {"x": x, "c": c, "w_ada": w_ada, "b_ada": b_ada, "norm1_w": norm1_w, "w_in": w_in,
            "dn_conv_w": dn_conv_w, "dn_A_log": dn_A_log, "dn_dt_bias": dn_dt_bias,
            "dn_norm_w": dn_norm_w, "w_proj_sb": w_proj_sb, "w_proj_dn": w_proj_dn, "w_out": w_out,
            "norm2_w": norm2_w, "w_ffn_in": w_ffn_in, "ffn_conv_w": ffn_conv_w,
            "ffn_conv_b": ffn_conv_b, "w_ffn_out": w_ffn_out, "final_norm_w": final_norm_w}


def reference(x, c, w_ada, b_ada, norm1_w, w_in, dn_conv_w, dn_A_log, dn_dt_bias, dn_norm_w,
              w_proj_sb, w_proj_dn, w_out, norm2_w, w_ffn_in, ffn_conv_w, ffn_conv_b, w_ffn_out,
              final_norm_w):
    B, S, _ = x.shape
    c_act = jax.nn.silu(c)
    for l in range(DEPTH):
        mod = (c_act @ w_ada[l] + b_ada[l])[:, None, :]
        shift1, scale1, gate1, shift2, scale2, gate2 = jnp.split(mod, 6, axis=-1)

        h = modulate(rmsnorm(x, norm1_w[l]), shift1, scale1)
        proj = h @ w_in[l]
        sb_qkv, dn_qkv, dn_z, dn_b, dn_a, br_gates = jnp.split(proj, IN_SPLITS, axis=-1)

        sb_q, sb_k, sb_v = jnp.split(sb_qkv.reshape(B, S, 3, SB_HEADS, SB_HEAD_DIM), 3, axis=2)
        o_a = stick_breaking_attention(sb_q[:, :, 0], sb_k[:, :, 0], sb_v[:, :, 0]).reshape(B, S, SB_WIDTH)

        dn_qkv = jax.nn.silu(causal_dwconv(dn_qkv, dn_conv_w[l]))
        dq, dk_, dv_ = jnp.split(dn_qkv, (DN_QK_WIDTH, 2 * DN_QK_WIDTH), axis=-1)
        dq = dq.reshape(B, S, DN_HEADS, DN_KEY_DIM)
        dk_ = dk_.reshape(B, S, DN_HEADS, DN_KEY_DIM)
        dv_ = dv_.reshape(B, S, DN_HEADS, DN_VAL_DIM)
        beta = jax.nn.sigmoid(dn_b)
        g = -jnp.exp(dn_A_log[l]) * jax.nn.softplus(dn_a + dn_dt_bias[l])
        o_b = gated_delta_rule(dq, dk_, dv_, g, beta)
        o_b = rmsnorm(o_b, dn_norm_w[l]) * jax.nn.silu(dn_z.reshape(B, S, DN_HEADS, DN_VAL_DIM))
        o_b = o_b.reshape(B, S, DN_V_WIDTH)

        gate_a, gate_b = jnp.split(jax.nn.sigmoid(br_gates), 2, axis=-1)
        merged = gate_a * (o_a @ w_proj_sb[l]) + gate_b * (o_b @ w_proj_dn[l])
        x = x + gate1 * (merged @ w_out[l])

        h2 = modulate(rmsnorm(x, norm2_w[l]), shift2, scale2)
        u = causal_dwconv(h2 @ w_ffn_in[l], ffn_conv_w[l]) + ffn_conv_b[l]
        u_gate, u_up = jnp.split(u, 2, axis=-1)
        x = x + gate2 * ((jax.nn.silu(u_gate) * u_up) @ w_ffn_out[l])
    return rmsnorm(x, final_norm_w)
```

```python
import functools

import jax
import jax.numpy as jnp
from jax import lax
from jax.experimental import pallas as pl
from jax.experimental.pallas import tpu as pltpu

F32 = jnp.float32
BF16 = jnp.bfloat16

D_MODEL = 1024
SB_HEADS = 8
SB_HEAD_DIM = 64
SB_WIDTH = SB_HEADS * SB_HEAD_DIM
DN_HEADS = 8
DN_KEY_DIM = 64
DN_VAL_DIM = 128
DN_QK_WIDTH = DN_HEADS * DN_KEY_DIM
DN_V_WIDTH = DN_HEADS * DN_VAL_DIM
DN_CONV_CH = 2 * DN_QK_WIDTH + DN_V_WIDTH
DN_CONV_WIDTH = 4
DN_CHUNK = 64
D_FF = 2816
FFN_CONV_WIDTH = 3
NORM_EPS = 1e-6
L2_EPS = 1e-6

LANES = 128
HALO = 16

COL_DN = 0
COL_Z = DN_CONV_CH
COL_GATE = COL_Z + DN_V_WIDTH
COL_SB = COL_GATE + 2 * D_MODEL
PROJ_WIDTH = COL_SB + 3 * SB_WIDTH

VMEM_LIMIT = 56 * 1024 * 1024


def _dot(a, b):
    return jnp.dot(a, b, preferred_element_type=F32)


def _dot_nt(a, b):
    return lax.dot_general(a, b, (((1,), (1,)), ((), ())), preferred_element_type=F32)


def _dot_tn(a, b):
    return lax.dot_general(a, b, (((0,), (0,)), ((), ())), preferred_element_type=F32)


def _split2(a):
    hi = a.astype(BF16)
    lo = (a - hi.astype(F32)).astype(BF16)
    return hi, lo


def _split3(a):
    a1 = a.astype(BF16)
    r = a - a1.astype(F32)
    a2 = r.astype(BF16)
    a3 = (r - a2.astype(F32)).astype(BF16)
    return a1, a2, a3


def _dot_exact_lhs(m, b):
    b1, b2, b3 = _split3(b)
    return _dot(m, b1) + (_dot(m, b2) + _dot(m, b3))


def _dot_exact_rhs2(a, m):
    hi, lo = _split2(a)
    return _dot(hi, m) + _dot(lo, m)


def _dot3(a, b):
    a1, a2 = _split2(a)
    b1, b2 = _split2(b)
    return _dot(a1, b1) + (_dot(a1, b2) + _dot(a2, b1))


def _sigmoid(x):
    return 1.0 / (1.0 + jnp.exp(-x))


def _softplus(x):
    return jnp.maximum(x, 0.0) + jnp.log(1.0 + jnp.exp(-jnp.abs(x)))


def _ada_kernel(c_ref, w_ref, b_ref, o_ref):
    c = c_ref[...]
    ca = c * _sigmoid(c)
    o_ref[...] = _dot3(ca, w_ref[...]) + b_ref[...]


def _ada(c_pad, w_ada, b_ada):
    n = w_ada.shape[1]
    tn = 1536
    return pl.pallas_call(
        _ada_kernel,
        grid=(n // tn,),
        in_specs=[
            pl.BlockSpec((8, D_MODEL), lambda j: (0, 0)),
            pl.BlockSpec((D_MODEL, tn), lambda j: (0, j)),
            pl.BlockSpec((1, tn), lambda j: (0, j)),
        ],
        out_specs=pl.BlockSpec((8, tn), lambda j: (0, j)),
        out_shape=jax.ShapeDtypeStruct((8, n), F32),
        compiler_params=pltpu.CompilerParams(
            dimension_semantics=("arbitrary",), vmem_limit_bytes=VMEM_LIMIT),
        name="ada",
    )(c_pad, w_ada, b_ada)


def _inproj_kernel(x_ref, mod_ref, nw_ref, w_ref, wt_ref, o_ref, ot_ref, h_ref):
    @pl.when(pl.program_id(1) == 0)
    def _():
        x = x_ref[...]
        ms = jnp.mean(x * x, axis=-1, keepdims=True)
        y = x * lax.rsqrt(ms + NORM_EPS) * nw_ref[...]
        m = mod_ref[0]
        h = (y * (1.0 + m[1:2, :]) + m[0:1, :]).astype(BF16)
        h_ref[...] = h
        ot_ref[...] = _dot(h, wt_ref[...])

    o_ref[...] = _dot(h_ref[...], w_ref[...]).astype(o_ref.dtype)


def _inproj(x2, mod3, norm_w, w_main, w_tail, seq):
    t = x2.shape[0]
    tm, tn = 1024, 1664
    tiles_per_seq = seq // tm
    return pl.pallas_call(
        _inproj_kernel,
        grid=(t // tm, PROJ_WIDTH // tn),
        in_specs=[
            pl.BlockSpec((tm, D_MODEL), lambda i, j: (i, 0)),
            pl.BlockSpec((1, 6, D_MODEL), lambda i, j: (i // tiles_per_seq, 0, 0)),
            pl.BlockSpec((1, D_MODEL), lambda i, j: (0, 0)),
            pl.BlockSpec((D_MODEL, tn), lambda i, j: (0, j)),
            pl.BlockSpec((D_MODEL, LANES), lambda i, j: (0, 0)),
        ],
        out_specs=[
            pl.BlockSpec((tm, tn), lambda i, j: (i, j)),
            pl.BlockSpec((tm, LANES), lambda i, j: (i, 0)),
        ],
        out_shape=[
            jax.ShapeDtypeStruct((t, PROJ_WIDTH), BF16),
            jax.ShapeDtypeStruct((t, LANES), F32),
        ],
        scratch_shapes=[pltpu.VMEM((tm, D_MODEL), BF16)],
        compiler_params=pltpu.CompilerParams(
            dimension_semantics=("parallel", "arbitrary"), vmem_limit_bytes=VMEM_LIMIT),
        name="inproj",
    )(x2, mod3, norm_w, w_main, w_tail)


SB_T = 256


def _sb_kernel(q_ref, k_ref, v_ref, o_ref):
    i = pl.program_id(2)
    q = q_ref[...]
    lane = lax.broadcasted_iota(jnp.int32, (1, LANES), 1)
    first = lane < SB_HEAD_DIM
    qs = q * jnp.asarray(SB_HEAD_DIM ** -0.5, BF16)
    zero = jnp.zeros_like(qs)
    q0 = jnp.where(first, qs, zero)
    q1 = jnp.where(first, zero, qs)
    r = lax.broadcasted_iota(jnp.int32, (SB_T, SB_T), 0)
    c = lax.broadcasted_iota(jnp.int32, (SB_T, SB_T), 1)
    upper = jnp.where(r > c, 1.0, 0.0).astype(BF16)
    causal = c < r

    def head_block(qh, kb, vb, carry, mask):
        z = _dot_nt(qh, kb)
        sp = _softplus(z)
        l1 = -sp
        if mask is not None:
            l1 = jnp.where(mask, l1, 0.0)
        cs = _dot_exact_rhs2(l1, upper)
        a = jnp.exp((z - sp) + cs + carry)
        if mask is not None:
            a = jnp.where(mask, a, 0.0)
        o = _dot(a.astype(BF16), vb)
        carry = carry + jnp.sum(l1, axis=-1, keepdims=True)
        return o, carry

    off_d = pl.multiple_of(i * SB_T, SB_T)
    k_d = k_ref[pl.ds(off_d, SB_T), :]
    v_d = v_ref[pl.ds(off_d, SB_T), :]
    c_init = jnp.zeros((SB_T, 1), F32)
    o0, c0 = head_block(q0, k_d, v_d, c_init, causal)
    o1, c1 = head_block(q1, k_d, v_d, c_init, causal)

    def body(n, st):
        o0, c0, o1, c1 = st
        off = pl.multiple_of((i - 1 - n) * SB_T, SB_T)
        kb = k_ref[pl.ds(off, SB_T), :]
        vb = v_ref[pl.ds(off, SB_T), :]
        a0, c0 = head_block(q0, kb, vb, c0, None)
        a1, c1 = head_block(q1, kb, vb, c1, None)
        return o0 + a0, c0, o1 + a1, c1

    o0, c0, o1, c1 = lax.fori_loop(0, i, body, (o0, c0, o1, c1))
    o_ref[...] = jnp.where(first, o0, o1).astype(o_ref.dtype)


def _sb_attention(proj, batch, seq):
    t = proj.shape[0]
    nq = seq // SB_T
    pairs = SB_HEADS // 2
    qcol = COL_SB // LANES
    kcol = qcol + SB_WIDTH // LANES
    vcol = kcol + SB_WIDTH // LANES
    return pl.pallas_call(
        _sb_kernel,
        grid=(batch, pairs, nq),
        in_specs=[
            pl.BlockSpec((SB_T, LANES), lambda b, p, i: (b * nq + i, qcol + p)),
            pl.BlockSpec((seq, LANES), lambda b, p, i: (b, kcol + p)),
            pl.BlockSpec((seq, LANES), lambda b, p, i: (b, vcol + p)),
        ],
        out_specs=pl.BlockSpec((SB_T, LANES), lambda b, p, i: (b * nq + i, p)),
        out_shape=jax.ShapeDtypeStruct((t, SB_WIDTH), BF16),
        compiler_params=pltpu.CompilerParams(
            dimension_semantics=("parallel", "parallel", "arbitrary"),
            vmem_limit_bytes=VMEM_LIMIT),
        name="sb",
    )(proj, proj, proj)


DNP_T = 512


def _dnprep_kernel(cur_ref, halo_ref, tail_ref, cw_ref, alog_ref, dtb_ref,
                   q_ref, k_ref, v_ref, bg_ref, xs_ref, *, tiles_per_seq):
    i = pl.program_id(0)
    cur = cur_ref[...].astype(F32)
    prev = halo_ref[...].astype(F32)
    prev = jnp.where(i % tiles_per_seq == 0, 0.0, prev)
    xs_ref[0:HALO, :] = prev
    xs_ref[HALO:, :] = cur
    w = cw_ref[...]
    y = w[DN_CONV_WIDTH - 1:DN_CONV_WIDTH, :] * cur
    for d in range(1, DN_CONV_WIDTH):
        y = y + w[DN_CONV_WIDTH - 1 - d:DN_CONV_WIDTH - d, :] * xs_ref[pl.ds(HALO - d, DNP_T), :]
    s = y * _sigmoid(y)

    r = lax.broadcasted_iota(jnp.int32, (DN_QK_WIDTH, DN_QK_WIDTH), 0) // DN_KEY_DIM
    c = lax.broadcasted_iota(jnp.int32, (DN_QK_WIDTH, DN_QK_WIDTH), 1) // DN_KEY_DIM
    same_head = jnp.where(r == c, 1.0, 0.0).astype(BF16)

    def l2n(t):
        ss = _dot_exact_rhs2(t * t, same_head)
        return t * lax.rsqrt(ss + L2_EPS)

    q = s[:, 0:DN_QK_WIDTH]
    k = s[:, DN_QK_WIDTH:2 * DN_QK_WIDTH]
    q_ref[...] = (l2n(q) * (DN_KEY_DIM ** -0.5)).astype(q_ref.dtype)
    k_ref[...] = l2n(k).astype(k_ref.dtype)
    v_ref[...] = s[:, 2 * DN_QK_WIDTH:].astype(v_ref.dtype)

    tl = tail_ref[...]
    lane = lax.broadcasted_iota(jnp.int32, (1, LANES), 1)
    beta = _sigmoid(tl)
    g = -jnp.exp(alog_ref[...]) * _softplus(tl + dtb_ref[...])
    bg_ref[...] = jnp.where(lane < DN_HEADS, beta, g)


def _dnprep(proj, tail, conv_w, alog_pad, dtb_pad, seq):
    t = proj.shape[0]
    tiles_per_seq = seq // DNP_T
    halo_blocks = DNP_T // HALO
    return pl.pallas_call(
        functools.partial(_dnprep_kernel, tiles_per_seq=tiles_per_seq),
        grid=(t // DNP_T,),
        in_specs=[
            pl.BlockSpec((DNP_T, DN_CONV_CH), lambda i: (i, 0)),
            pl.BlockSpec((HALO, DN_CONV_CH), lambda i: (jnp.maximum(i * halo_blocks - 1, 0), 0)),
            pl.BlockSpec((DNP_T, LANES), lambda i: (i, 0)),
            pl.BlockSpec((DN_CONV_WIDTH, DN_CONV_CH), lambda i: (0, 0)),
            pl.BlockSpec((1, LANES), lambda i: (0, 0)),
            pl.BlockSpec((1, LANES), lambda i: (0, 0)),
        ],
        out_specs=[
            pl.BlockSpec((DNP_T, DN_QK_WIDTH), lambda i: (i, 0)),
            pl.BlockSpec((DNP_T, DN_QK_WIDTH), lambda i: (i, 0)),
            pl.BlockSpec((DNP_T, DN_V_WIDTH), lambda i: (i, 0)),
            pl.BlockSpec((DNP_T, LANES), lambda i: (i, 0)),
        ],
        out_shape=[
            jax.ShapeDtypeStruct((t, DN_QK_WIDTH), BF16),
            jax.ShapeDtypeStruct((t, DN_QK_WIDTH), BF16),
            jax.ShapeDtypeStruct((t, DN_V_WIDTH), BF16),
            jax.ShapeDtypeStruct((t, LANES), F32),
        ],
        scratch_shapes=[pltpu.VMEM((DNP_T + HALO, DN_CONV_CH), F32)],
        compiler_params=pltpu.CompilerParams(
            dimension_semantics=("parallel",), vmem_limit_bytes=VMEM_LIMIT),
        name="dnprep",
    )(proj, proj, tail, conv_w, alog_pad, dtb_pad)


def _dnchunk_kernel(q_ref, k_ref, v_ref, bg_ref, z_ref, nw_ref, o_ref, state_ref):
    C = DN_CHUNK
    n = pl.program_id(1)

    @pl.when(n == 0)
    def _():
        state_ref[...] = jnp.zeros_like(state_ref)

    ri = lax.broadcasted_iota(jnp.int32, (C, C), 0)
    ci = lax.broadcasted_iota(jnp.int32, (C, C), 1)
    incl = ri >= ci
    strict = ri > ci
    lower_incl = jnp.where(incl, 1.0, 0.0).astype(BF16)
    after = jnp.where(strict, 1.0, 0.0)
    eye = jnp.where(ri == ci, 1.0, 0.0)
    blk = {b: (ri // b) == (ci // b) for b in (8, 16, 32, 64)}
    lane = lax.broadcasted_iota(jnp.int32, (1, LANES), 1)
    first = lane < DN_KEY_DIM
    row2 = lax.broadcasted_iota(jnp.int32, (2 * DN_KEY_DIM, 1), 0)
    first_rows = row2 < DN_KEY_DIM

    bg = bg_ref[...]
    nw = nw_ref[...]

    for p in range(DN_HEADS // 2):
        kp = k_ref[:, p * LANES:(p + 1) * LANES]
        qp = q_ref[:, p * LANES:(p + 1) * LANES]
        kpf = kp.astype(F32)
        qpf = qp.astype(F32)
        s_pair = state_ref[p]
        s_bf = s_pair.astype(BF16)
        kv_sum = jnp.zeros((2 * DN_KEY_DIM, DN_VAL_DIM), F32)
        glast_rows = []
        for e in range(2):
            h = 2 * p + e
            sel = first if e == 0 else jnp.logical_not(first)
            beta_b = jnp.broadcast_to(bg[:, h:h + 1], (C, LANES))
            g_b = jnp.broadcast_to(bg[:, DN_HEADS + h:DN_HEADS + h + 1], (C, LANES))
            gc_b = _dot_exact_lhs(lower_incl, g_b)
            diff = _dot_exact_lhs(lower_incl, g_b[:, :C] * after)
            decay = jnp.where(incl, jnp.exp(diff), 0.0)
            km = jnp.where(sel, kp, jnp.zeros_like(kp))
            qm = jnp.where(sel, qp, jnp.zeros_like(qp))
            kk = _dot_nt(km, kp)
            qk = _dot_nt(qm, kp)
            lmat = jnp.where(strict, kk * beta_b[:, :C] * decay, 0.0)
            pw = jnp.where(blk[8], -lmat, 0.0)
            tinv = eye + pw
            for _ in range(2):
                pw = _dot3(pw, pw)
                tinv = tinv + _dot3(tinv, pw)
            for b in (8, 16, 32):
                cpart = jnp.where(jnp.logical_and(blk[2 * b], jnp.logical_not(blk[b])), lmat, 0.0)
                tinv = tinv - _dot3(tinv, _dot3(cpart, tinv))
            tb = tinv.astype(BF16)
            vh = v_ref[:, h * DN_VAL_DIM:(h + 1) * DN_VAL_DIM].astype(F32)
            u = _dot(tb, (vh * beta_b).astype(BF16))
            egc = jnp.exp(gc_b)
            kmf = jnp.where(sel, kpf, 0.0)
            w = _dot(tb, (kmf * beta_b * egc).astype(BF16))
            intra = jnp.where(incl, qk * decay, 0.0)
            v_new = u - _dot(w.astype(BF16), s_bf)
            v_new_b = v_new.astype(BF16)
            qg = (jnp.where(sel, qpf, 0.0) * egc).astype(BF16)
            o = _dot(qg, s_bf) + _dot(intra.astype(BF16), v_new_b)
            g_last = gc_b[C - 1:C, :]
            k_dec = (kmf * jnp.exp(g_last - gc_b)).astype(BF16)
            kv_sum = kv_sum + _dot_tn(k_dec, v_new_b)
            glast_rows.append(jnp.broadcast_to(g_last, (2 * DN_KEY_DIM, LANES)))
            ms = jnp.mean(o * o, axis=-1, keepdims=True)
            zh = z_ref[:, h * DN_VAL_DIM:(h + 1) * DN_VAL_DIM].astype(F32)
            on = o * lax.rsqrt(ms + NORM_EPS) * nw * (zh * _sigmoid(zh))
            o_ref[:, h * DN_VAL_DIM:(h + 1) * DN_VAL_DIM] = on.astype(o_ref.dtype)
        eg = jnp.exp(jnp.where(first_rows, glast_rows[0], glast_rows[1]))
        state_ref[p] = s_pair * eg + kv_sum


def _dnchunk(qn, kn, v, bg, proj, dn_norm_w, batch, seq):
    t = qn.shape[0]
    nchunks = seq // DN_CHUNK
    zcol = COL_Z // DN_V_WIDTH
    row = lambda b, n: (b * nchunks + n, 0)
    return pl.pallas_call(
        _dnchunk_kernel,
        grid=(batch, nchunks),
        in_specs=[
            pl.BlockSpec((DN_CHUNK, DN_QK_WIDTH), row),
            pl.BlockSpec((DN_CHUNK, DN_QK_WIDTH), row),
            pl.BlockSpec((DN_CHUNK, DN_V_WIDTH), row),
            pl.BlockSpec((DN_CHUNK, LANES), row),
            pl.BlockSpec((DN_CHUNK, DN_V_WIDTH), lambda b, n: (b * nchunks + n, zcol)),
            pl.BlockSpec((1, DN_VAL_DIM), lambda b, n: (0, 0)),
        ],
        out_specs=pl.BlockSpec((DN_CHUNK, DN_V_WIDTH), row),
        out_shape=jax.ShapeDtypeStruct((t, DN_V_WIDTH), BF16),
        scratch_shapes=[pltpu.VMEM((DN_HEADS // 2, 2 * DN_KEY_DIM, DN_VAL_DIM), F32)],
        compiler_params=pltpu.CompilerParams(
            dimension_semantics=("parallel", "arbitrary"), vmem_limit_bytes=VMEM_LIMIT),
        name="dnchunk",
    )(qn, kn, v, bg, proj, dn_norm_w)


MG_T = 512


def _merge_kernel(oa_ref, ob_ref, ga_ref, gb_ref, x_ref, mod_ref, wsb_ref, wdn_ref, wout_ref,
                  nw_ref, x1_ref, h2_ref):
    pa = _dot(oa_ref[...], wsb_ref[...])
    pb = _dot(ob_ref[...], wdn_ref[...])
    merged = _sigmoid(ga_ref[...].astype(F32)) * pa + _sigmoid(gb_ref[...].astype(F32)) * pb
    y = _dot(merged.astype(BF16), wout_ref[...])
    m = mod_ref[0]
    x1 = x_ref[...] + m[2:3, :] * y
    x1_ref[...] = x1
    ms = jnp.mean(x1 * x1, axis=-1, keepdims=True)
    hn = x1 * lax.rsqrt(ms + NORM_EPS) * nw_ref[...]
    h2_ref[...] = (hn * (1.0 + m[4:5, :]) + m[3:4, :]).astype(h2_ref.dtype)


def _merge(o_a, o_b, proj, x2, mod3, w_sb, w_dn, w_out, norm2_w, seq):
    t = x2.shape[0]
    tiles_per_seq = seq // MG_T
    gcol = COL_GATE // D_MODEL
    const = lambda i: (0, 0)
    return pl.pallas_call(
        _merge_kernel,
        grid=(t // MG_T,),
        in_specs=[
            pl.BlockSpec((MG_T, SB_WIDTH), lambda i: (i, 0)),
            pl.BlockSpec((MG_T, DN_V_WIDTH), lambda i: (i, 0)),
            pl.BlockSpec((MG_T, D_MODEL), lambda i: (i, gcol)),
            pl.BlockSpec((MG_T, D_MODEL), lambda i: (i, gcol + 1)),
            pl.BlockSpec((MG_T, D_MODEL), lambda i: (i, 0)),
            pl.BlockSpec((1, 6, D_MODEL), lambda i: (i // tiles_per_seq, 0, 0)),
            pl.BlockSpec((SB_WIDTH, D_MODEL), const),
            pl.BlockSpec((DN_V_WIDTH, D_MODEL), const),
            pl.BlockSpec((D_MODEL, D_MODEL), const),
            pl.BlockSpec((1, D_MODEL), const),
        ],
        out_specs=[
            pl.BlockSpec((MG_T, D_MODEL), lambda i: (i, 0)),
            pl.BlockSpec((MG_T, D_MODEL), lambda i: (i, 0)),
        ],
        out_shape=[
            jax.ShapeDtypeStruct((t, D_MODEL), F32),
            jax.ShapeDtypeStruct((t, D_MODEL), BF16),
        ],
        compiler_params=pltpu.CompilerParams(
            dimension_semantics=("parallel",), vmem_limit_bytes=VMEM_LIMIT),
        name="merge",
    )(o_a, o_b, proj, proj, x2, mod3, w_sb, w_dn, w_out, norm2_w)


FI_TM = 1024
FI_TN = 256


def _ffnin_kernel(h_ref, halo_ref, wg_ref, wu_ref, cwg_ref, cwu_ref, bg_ref, bu_ref, o_ref,
                  sg_ref, su_ref, *, tiles_per_seq):
    i = pl.program_id(0)
    h = h_ref[...]
    hp = halo_ref[...]
    keep = jnp.where(i % tiles_per_seq == 0, 0.0, 1.0)

    def branch(w_ref, cw_ref, b_ref, s_ref):
        u = _dot(h, w_ref[...])
        s_ref[0:HALO, :] = _dot(hp, w_ref[...]) * keep
        s_ref[HALO:, :] = u
        cw = cw_ref[...]
        y = cw[FFN_CONV_WIDTH - 1:FFN_CONV_WIDTH, :] * u + b_ref[...]
        for d in range(1, FFN_CONV_WIDTH):
            y = y + cw[FFN_CONV_WIDTH - 1 - d:FFN_CONV_WIDTH - d, :] * s_ref[pl.ds(HALO - d, FI_TM), :]
        return y

    yg = branch(wg_ref, cwg_ref, bg_ref, sg_ref)
    yu = branch(wu_ref, cwu_ref, bu_ref, su_ref)
    o_ref[...] = (yg * _sigmoid(yg) * yu).astype(o_ref.dtype)


def _ffnin(h2, w_ffn_in, conv_w, conv_b, seq):
    t = h2.shape[0]
    tiles_per_seq = seq // FI_TM
    halo_blocks = FI_TM // HALO
    ncol = D_FF // FI_TN
    return pl.pallas_call(
        functools.partial(_ffnin_kernel, tiles_per_seq=tiles_per_seq),
        grid=(t // FI_TM, ncol),
        in_specs=[
            pl.BlockSpec((FI_TM, D_MODEL), lambda i, j: (i, 0)),
            pl.BlockSpec((HALO, D_MODEL), lambda i, j: (jnp.maximum(i * halo_blocks - 1, 0), 0)),
            pl.BlockSpec((D_MODEL, FI_TN), lambda i, j: (0, j)),
            pl.BlockSpec((D_MODEL, FI_TN), lambda i, j: (0, ncol + j)),
            pl.BlockSpec((FFN_CONV_WIDTH, FI_TN), lambda i, j: (0, j)),
            pl.BlockSpec((FFN_CONV_WIDTH, FI_TN), lambda i, j: (0, ncol + j)),
            pl.BlockSpec((1, FI_TN), lambda i, j: (0, j)),
            pl.BlockSpec((1, FI_TN), lambda i, j: (0, ncol + j)),
        ],
        out_specs=pl.BlockSpec((FI_TM, FI_TN), lambda i, j: (i, j)),
        out_shape=jax.ShapeDtypeStruct((t, D_FF), BF16),
        scratch_shapes=[pltpu.VMEM((FI_TM + HALO, FI_TN), F32),
                        pltpu.VMEM((FI_TM + HALO, FI_TN), F32)],
        compiler_params=pltpu.CompilerParams(
            dimension_semantics=("parallel", "arbitrary"), vmem_limit_bytes=VMEM_LIMIT),
        name="ffnin",
    )(h2, h2, w_ffn_in, w_ffn_in, conv_w, conv_w, conv_b, conv_b)


FO_T = 512


def _ffnout_kernel(a_ref, w_ref, x1_ref, mod_ref, nw_ref, o_ref):
    y = _dot(a_ref[...], w_ref[...])
    m = mod_ref[0]
    x2 = x1_ref[...] + m[5:6, :] * y
    ms = jnp.mean(x2 * x2, axis=-1, keepdims=True)
    o_ref[...] = x2 * lax.rsqrt(ms + NORM_EPS) * nw_ref[...]


def _ffnout(act, w_ffn_out, x1, mod3, final_w, seq):
    t = x1.shape[0]
    tiles_per_seq = seq // FO_T
    return pl.pallas_call(
        _ffnout_kernel,
        grid=(t // FO_T,),
        in_specs=[
            pl.BlockSpec((FO_T, D_FF), lambda i: (i, 0)),
            pl.BlockSpec((D_FF, D_MODEL), lambda i: (0, 0)),
            pl.BlockSpec((FO_T, D_MODEL), lambda i: (i, 0)),
            pl.BlockSpec((1, 6, D_MODEL), lambda i: (i // tiles_per_seq, 0, 0)),
            pl.BlockSpec((1, D_MODEL), lambda i: (0, 0)),
        ],
        out_specs=pl.BlockSpec((FO_T, D_MODEL), lambda i: (i, 0)),
        out_shape=jax.ShapeDtypeStruct((t, D_MODEL), F32),
        compiler_params=pltpu.CompilerParams(
            dimension_semantics=("parallel",), vmem_limit_bytes=VMEM_LIMIT),
        name="ffnout",
    )(act, w_ffn_out, x1, mod3, final_w)


def kernel(x, c, w_ada, b_ada, norm1_w, w_in, dn_conv_w, dn_A_log, dn_dt_bias, dn_norm_w,
           w_proj_sb, w_proj_dn, w_out, norm2_w, w_ffn_in, ffn_conv_w, ffn_conv_b, w_ffn_out,
           final_norm_w):
    batch, seq, d = x.shape
    depth = w_ada.shape[0]
    assert depth == 1, "the final rmsnorm is fused into the last layer's FFN-out kernel"
    t = batch * seq
    xt = x.reshape(t, d)
    c_pad = jnp.pad(c, ((0, 8 - batch), (0, 0)))

    sb_end = 3 * SB_WIDTH
    dn_end = sb_end + DN_CONV_CH
    z_end = dn_end + DN_V_WIDTH
    tail_end = z_end + 2 * DN_HEADS

    for l in range(depth):
        wl = w_in[l]
        w_main = jnp.concatenate(
            [wl[:, sb_end:dn_end], wl[:, dn_end:z_end], wl[:, tail_end:], wl[:, :sb_end]],
            axis=1).astype(BF16)
        w_tail = jnp.pad(wl[:, z_end:tail_end], ((0, 0), (0, LANES - 2 * DN_HEADS))).astype(BF16)
        alog_pad = jnp.pad(dn_A_log[l], (DN_HEADS, LANES - 2 * DN_HEADS)).reshape(1, LANES)
        dtb_pad = jnp.pad(dn_dt_bias[l], (DN_HEADS, LANES - 2 * DN_HEADS)).reshape(1, LANES)

        mod = _ada(c_pad, w_ada[l], b_ada[l].reshape(1, -1))
        mod3 = mod[:batch].reshape(batch, 6, d)

        proj, tail = _inproj(xt, mod3, norm1_w[l].reshape(1, d), w_main, w_tail, seq)
        o_a = _sb_attention(proj, batch, seq)
        qn, kn, vv, bg = _dnprep(proj, tail, dn_conv_w[l], alog_pad, dtb_pad, seq)
        o_b = _dnchunk(qn, kn, vv, bg, proj, dn_norm_w[l].reshape(1, -1), batch, seq)
        x1, h2 = _merge(o_a, o_b, proj, xt, mod3, w_proj_sb[l].astype(BF16),
                        w_proj_dn[l].astype(BF16), w_out[l].astype(BF16),
                        norm2_w[l].reshape(1, d), seq)
        act = _ffnin(h2, w_ffn_in[l].astype(BF16), ffn_conv_w[l], ffn_conv_b[l].reshape(1, -1), seq)
        xt = _ffnout(act, w_ffn_out[l].astype(BF16), x1, mod3, final_norm_w.reshape(1, d), seq)
    return xt.reshape(batch, seq, d)
```

```python
import functools

import jax
import jax.numpy as jnp
from jax import lax
from jax.experimental import pallas as pl
from jax.experimental.pallas import tpu as pltpu

F32 = jnp.float32
BF16 = jnp.bfloat16

D_MODEL = 1024
SB_HEADS = 8
SB_HEAD_DIM = 64
SB_WIDTH = SB_HEADS * SB_HEAD_DIM
DN_HEADS = 8
DN_KEY_DIM = 64
DN_VAL_DIM = 128
DN_QK_WIDTH = DN_HEADS * DN_KEY_DIM
DN_V_WIDTH = DN_HEADS * DN_VAL_DIM
DN_CONV_CH = 2 * DN_QK_WIDTH + DN_V_WIDTH
DN_CONV_WIDTH = 4
DN_CHUNK = 64
D_FF = 2816
FFN_CONV_WIDTH = 3
NORM_EPS = 1e-6
L2_EPS = 1e-6

LANES = 128
HALO = 16

COL_DN = 0
COL_Z = DN_CONV_CH
COL_GATE = COL_Z + DN_V_WIDTH
COL_SB = COL_GATE + 2 * D_MODEL
PROJ_WIDTH = COL_SB + 3 * SB_WIDTH

VMEM_LIMIT = 56 * 1024 * 1024


def _dot(a, b):
    return jnp.dot(a, b, preferred_element_type=F32)


def _dot_nt(a, b):
    return lax.dot_general(a, b, (((1,), (1,)), ((), ())), preferred_element_type=F32)


def _dot_tn(a, b):
    return lax.dot_general(a, b, (((0,), (0,)), ((), ())), preferred_element_type=F32)


def _split2(a):
    hi = a.astype(BF16)
    lo = (a - hi.astype(F32)).astype(BF16)
    return hi, lo


def _split3(a):
    a1 = a.astype(BF16)
    r = a - a1.astype(F32)
    a2 = r.astype(BF16)
    a3 = (r - a2.astype(F32)).astype(BF16)
    return a1, a2, a3


def _dot_exact_lhs(m, b):
    b1, b2, b3 = _split3(b)
    return _dot(m, b1) + (_dot(m, b2) + _dot(m, b3))


def _dot_exact_rhs2(a, m):
    hi, lo = _split2(a)
    return _dot(hi, m) + _dot(lo, m)


def _dot3(a, b):
    a1, a2 = _split2(a)
    b1, b2 = _split2(b)
    return _dot(a1, b1) + (_dot(a1, b2) + _dot(a2, b1))


def _sigmoid(x):
    return 1.0 / (1.0 + jnp.exp(-x))


def _softplus(x):
    return jnp.maximum(x, 0.0) + jnp.log(1.0 + jnp.exp(-jnp.abs(x)))


def _ada_kernel(c_ref, w_ref, b_ref, o_ref):
    c = c_ref[...]
    ca = c * _sigmoid(c)
    o_ref[...] = _dot3(ca, w_ref[...]) + b_ref[...]


def _ada(c_pad, w_ada, b_ada):
    n = w_ada.shape[1]
    tn = 1536
    return pl.pallas_call(
        _ada_kernel,
        grid=(n // tn,),
        in_specs=[
            pl.BlockSpec((8, D_MODEL), lambda j: (0, 0)),
            pl.BlockSpec((D_MODEL, tn), lambda j: (0, j)),
            pl.BlockSpec((1, tn), lambda j: (0, j)),
        ],
        out_specs=pl.BlockSpec((8, tn), lambda j: (0, j)),
        out_shape=jax.ShapeDtypeStruct((8, n), F32),
        compiler_params=pltpu.CompilerParams(
            dimension_semantics=("arbitrary",), vmem_limit_bytes=VMEM_LIMIT),
        name="ada",
    )(c_pad, w_ada, b_ada)


def _inproj_kernel(x_ref, mod_ref, nw_ref, w_ref, wt_ref, o_ref, ot_ref, h_ref):
    @pl.when(pl.program_id(1) == 0)
    def _():
        x = x_ref[...]
        ms = jnp.mean(x * x, axis=-1, keepdims=True)
        y = x * lax.rsqrt(ms + NORM_EPS) * nw_ref[...]
        m = mod_ref[0]
        h = (y * (1.0 + m[1:2, :]) + m[0:1, :]).astype(BF16)
        h_ref[...] = h
        ot_ref[...] = _dot(h, wt_ref[...])

    o_ref[...] = _dot(h_ref[...], w_ref[...]).astype(o_ref.dtype)


def _inproj(x2, mod3, norm_w, w_main, w_tail, seq):
    t = x2.shape[0]
    tm, tn = 1024, 1664
    tiles_per_seq = seq // tm
    return pl.pallas_call(
        _inproj_kernel,
        grid=(t // tm, PROJ_WIDTH // tn),
        in_specs=[
            pl.BlockSpec((tm, D_MODEL), lambda i, j: (i, 0)),
            pl.BlockSpec((1, 6, D_MODEL), lambda i, j: (i // tiles_per_seq, 0, 0)),
            pl.BlockSpec((1, D_MODEL), lambda i, j: (0, 0)),
            pl.BlockSpec((D_MODEL, tn), lambda i, j: (0, j)),
            pl.BlockSpec((D_MODEL, LANES), lambda i, j: (0, 0)),
        ],
        out_specs=[
            pl.BlockSpec((tm, tn), lambda i, j: (i, j)),
            pl.BlockSpec((tm, LANES), lambda i, j: (i, 0)),
        ],
        out_shape=[
            jax.ShapeDtypeStruct((t, PROJ_WIDTH), BF16),
            jax.ShapeDtypeStruct((t, LANES), F32),
        ],
        scratch_shapes=[pltpu.VMEM((tm, D_MODEL), BF16)],
        compiler_params=pltpu.CompilerParams(
            dimension_semantics=("parallel", "arbitrary"), vmem_limit_bytes=VMEM_LIMIT),
        name="inproj",
    )(x2, mod3, norm_w, w_main, w_tail)


SB_T = 256
SB_DEAD_LOG = -100.0


def _sb_kernel(q_ref, k_ref, v_ref, o_ref):
    i = pl.program_id(2)
    q = q_ref[...]
    lane = lax.broadcasted_iota(jnp.int32, (1, LANES), 1)
    first = lane < SB_HEAD_DIM
    qs = q * jnp.asarray(SB_HEAD_DIM ** -0.5, BF16)
    zero = jnp.zeros_like(qs)
    q2 = jnp.concatenate([jnp.where(first, qs, zero), jnp.where(first, zero, qs)], axis=0)
    r = lax.broadcasted_iota(jnp.int32, (SB_T, SB_T), 0)
    c = lax.broadcasted_iota(jnp.int32, (SB_T, SB_T), 1)
    upper = jnp.where(r > c, 1.0, 0.0).astype(BF16)
    r2 = lax.broadcasted_iota(jnp.int32, (2 * SB_T, SB_T), 0) % SB_T
    c2 = lax.broadcasted_iota(jnp.int32, (2 * SB_T, SB_T), 1)
    causal = c2 < r2

    def block(kb, vb, carry, mask):
        z = _dot_nt(q2, kb)
        sp = _softplus(z)
        l1 = -sp
        if mask is not None:
            l1 = jnp.where(mask, l1, 0.0)
        cs = _dot_exact_rhs2(l1, upper)
        a = jnp.exp((z - sp) + cs + carry)
        if mask is not None:
            a = jnp.where(mask, a, 0.0)
        o = _dot(a.astype(BF16), vb)
        carry = carry + jnp.sum(l1, axis=-1, keepdims=True)
        return o, carry

    off_d = pl.multiple_of(i * SB_T, SB_T)
    o, carry = block(k_ref[pl.ds(off_d, SB_T), :], v_ref[pl.ds(off_d, SB_T), :],
                     jnp.zeros((2 * SB_T, 1), F32), causal)

    def live(carry):
        return jnp.max(carry) > SB_DEAD_LOG

    def cond(st):
        n, go = st[0], st[1]
        return jnp.logical_and(n < i, go)

    def body(st):
        n, _, o, carry = st
        off = pl.multiple_of((i - 1 - n) * SB_T, SB_T)
        a, carry = block(k_ref[pl.ds(off, SB_T), :], v_ref[pl.ds(off, SB_T), :], carry, None)
        return n + 1, live(carry), o + a, carry

    st = lax.while_loop(cond, body, (jnp.int32(0), live(carry), o, carry))
    o_ref[...] = jnp.where(first, st[2][:SB_T], st[2][SB_T:]).astype(o_ref.dtype)


def _sb_attention(proj, batch, seq):
    t = proj.shape[0]
    nq = seq // SB_T
    pairs = SB_HEADS // 2
    qcol = COL_SB // LANES
    kcol = qcol + SB_WIDTH // LANES
    vcol = kcol + SB_WIDTH // LANES
    return pl.pallas_call(
        _sb_kernel,
        grid=(batch, pairs, nq),
        in_specs=[
            pl.BlockSpec((SB_T, LANES), lambda b, p, i: (b * nq + i, qcol + p)),
            pl.BlockSpec((seq, LANES), lambda b, p, i: (b, kcol + p)),
            pl.BlockSpec((seq, LANES), lambda b, p, i: (b, vcol + p)),
        ],
        out_specs=pl.BlockSpec((SB_T, LANES), lambda b, p, i: (b * nq + i, p)),
        out_shape=jax.ShapeDtypeStruct((t, SB_WIDTH), BF16),
        compiler_params=pltpu.CompilerParams(
            dimension_semantics=("parallel", "parallel", "arbitrary"),
            vmem_limit_bytes=VMEM_LIMIT),
        name="sb",
    )(proj, proj, proj)


DNP_T = 512


def _dnprep_kernel(cur_ref, halo_ref, tail_ref, cw_ref, alog_ref, dtb_ref,
                   q_ref, k_ref, v_ref, bg_ref, xs_ref, *, tiles_per_seq):
    i = pl.program_id(0)
    cur = cur_ref[...].astype(F32)
    prev = halo_ref[...].astype(F32)
    prev = jnp.where(i % tiles_per_seq == 0, 0.0, prev)
    xs_ref[0:HALO, :] = prev
    xs_ref[HALO:, :] = cur
    w = cw_ref[...]
    y = w[DN_CONV_WIDTH - 1:DN_CONV_WIDTH, :] * cur
    for d in range(1, DN_CONV_WIDTH):
        y = y + w[DN_CONV_WIDTH - 1 - d:DN_CONV_WIDTH - d, :] * xs_ref[pl.ds(HALO - d, DNP_T), :]
    s = y * _sigmoid(y)

    r = lax.broadcasted_iota(jnp.int32, (DN_QK_WIDTH, DN_QK_WIDTH), 0) // DN_KEY_DIM
    c = lax.broadcasted_iota(jnp.int32, (DN_QK_WIDTH, DN_QK_WIDTH), 1) // DN_KEY_DIM
    same_head = jnp.where(r == c, 1.0, 0.0).astype(BF16)

    def l2n(t):
        ss = _dot_exact_rhs2(t * t, same_head)
        return t * lax.rsqrt(ss + L2_EPS)

    q = s[:, 0:DN_QK_WIDTH]
    k = s[:, DN_QK_WIDTH:2 * DN_QK_WIDTH]
    q_ref[...] = (l2n(q) * (DN_KEY_DIM ** -0.5)).astype(q_ref.dtype)
    k_ref[...] = l2n(k).astype(k_ref.dtype)
    v_ref[...] = s[:, 2 * DN_QK_WIDTH:].astype(v_ref.dtype)

    tl = tail_ref[...]
    lane = lax.broadcasted_iota(jnp.int32, (1, LANES), 1)
    beta = _sigmoid(tl)
    g = -jnp.exp(alog_ref[...]) * _softplus(tl + dtb_ref[...])
    bg_ref[...] = jnp.where(lane < DN_HEADS, beta, g)


def _dnprep(proj, tail, conv_w, alog_pad, dtb_pad, seq):
    t = proj.shape[0]
    tiles_per_seq = seq // DNP_T
    halo_blocks = DNP_T // HALO
    return pl.pallas_call(
        functools.partial(_dnprep_kernel, tiles_per_seq=tiles_per_seq),
        grid=(t // DNP_T,),
        in_specs=[
            pl.BlockSpec((DNP_T, DN_CONV_CH), lambda i: (i, 0)),
            pl.BlockSpec((HALO, DN_CONV_CH), lambda i: (jnp.maximum(i * halo_blocks - 1, 0), 0)),
            pl.BlockSpec((DNP_T, LANES), lambda i: (i, 0)),
            pl.BlockSpec((DN_CONV_WIDTH, DN_CONV_CH), lambda i: (0, 0)),
            pl.BlockSpec((1, LANES), lambda i: (0, 0)),
            pl.BlockSpec((1, LANES), lambda i: (0, 0)),
        ],
        out_specs=[
            pl.BlockSpec((DNP_T, DN_QK_WIDTH), lambda i: (i, 0)),
            pl.BlockSpec((DNP_T, DN_QK_WIDTH), lambda i: (i, 0)),
            pl.BlockSpec((DNP_T, DN_V_WIDTH), lambda i: (i, 0)),
            pl.BlockSpec((DNP_T, LANES), lambda i: (i, 0)),
        ],
        out_shape=[
            jax.ShapeDtypeStruct((t, DN_QK_WIDTH), BF16),
            jax.ShapeDtypeStruct((t, DN_QK_WIDTH), BF16),
            jax.ShapeDtypeStruct((t, DN_V_WIDTH), BF16),
            jax.ShapeDtypeStruct((t, LANES), F32),
        ],
        scratch_shapes=[pltpu.VMEM((DNP_T + HALO, DN_CONV_CH), F32)],
        compiler_params=pltpu.CompilerParams(
            dimension_semantics=("parallel",), vmem_limit_bytes=VMEM_LIMIT),
        name="dnprep",
    )(proj, proj, tail, conv_w, alog_pad, dtb_pad)


DN_GROUP = 4
DN_GROUP_LANES = DN_GROUP * DN_KEY_DIM
DN_ROWS = DN_HEADS * DN_CHUNK
DNL_CHUNKS = 2


def _bd4(x, mask):
    return jnp.where(mask, jnp.concatenate([x, x, x, x], axis=0), jnp.zeros((), x.dtype))


def _dot3_bd(a, b, mask):
    a1, a2 = _split2(a)
    b1, b2 = _split2(b)
    bd1 = _bd4(b1, mask)
    bd2 = _bd4(b2, mask)
    return _dot(a1, bd1) + (_dot(a1, bd2) + _dot(a2, bd1))


def _dnlocal_kernel(q_ref, k_ref, v_ref, bg_ref, u_ref, wqk_ref, intra_ref, eg_ref):
    C = DN_CHUNK
    GL = DN_GROUP_LANES
    ri = lax.broadcasted_iota(jnp.int32, (C, GL), 0)
    cj = lax.broadcasted_iota(jnp.int32, (C, GL), 1) % C
    incl = ri >= cj
    strict = ri > cj
    after = jnp.where(strict, 1.0, 0.0)
    eye = jnp.where(ri == cj, 1.0, 0.0)
    blk = {b: (ri // b) == (cj // b) for b in (8, 16, 32, 64)}
    r2 = lax.broadcasted_iota(jnp.int32, (C, C), 0)
    c2 = lax.broadcasted_iota(jnp.int32, (C, C), 1)
    lower_incl = jnp.where(r2 >= c2, 1.0, 0.0).astype(BF16)
    rb = lax.broadcasted_iota(jnp.int32, (GL, GL), 0) // C
    cb = lax.broadcasted_iota(jnp.int32, (GL, GL), 1) // C
    bdmask = rb == cb
    lane = lax.broadcasted_iota(jnp.int32, (1, LANES), 1)
    first = lane < DN_KEY_DIM
    sels = (first, jnp.logical_not(first))

    chains = [(c, g) for c in range(DNL_CHUNKS) for g in range(DN_HEADS // DN_GROUP)]
    rows_of = lambda c: slice(c * C, (c + 1) * C)
    lanes_of = lambda g: slice(g * GL, (g + 1) * GL)
    each = lambda f, *ls: [f(*xs) for xs in zip(*ls)]

    def head_bcast(c, g, col0):
        bgc = bg_ref[rows_of(c), :]
        return [jnp.broadcast_to(bgc[:, col0 + h:col0 + h + 1], (C, LANES))
                for h in range(g * DN_GROUP, (g + 1) * DN_GROUP)]

    def cat(hs):
        return jnp.concatenate([jnp.where(first, hs[0], hs[1]), jnp.where(first, hs[2], hs[3])], axis=1)

    beta_h = [head_bcast(c, g, 0) for c, g in chains]
    beta_cat = each(cat, beta_h)
    g_cat = [cat(head_bcast(c, g, DN_HEADS)) for c, g in chains]
    k_cat = [k_ref[rows_of(c), lanes_of(g)] for c, g in chains]
    q_cat = [q_ref[rows_of(c), lanes_of(g)] for c, g in chains]

    gc = [_dot_exact_lhs(lower_incl, x) for x in g_cat]
    diff = [_dot_exact_lhs(lower_incl, x * after) for x in g_cat]
    kq = each(lambda k, q: _dot_nt(jnp.concatenate([k, q], axis=0), _bd4(k, bdmask)), k_cat, q_cat)
    decay = [jnp.where(incl, jnp.exp(x), 0.0) for x in diff]
    lmat = each(lambda x, b, d: jnp.where(strict, x[:C] * b * d, 0.0), kq, beta_cat, decay)
    for (c, g), x, d in zip(chains, kq, decay):
        intra_ref[rows_of(c), lanes_of(g)] = jnp.where(incl, x[C:] * d, 0.0).astype(intra_ref.dtype)

    pw = [jnp.where(blk[8], -x, 0.0) for x in lmat]
    tinv = [eye + x for x in pw]
    p2 = [_dot3_bd(x, x, bdmask) for x in pw]
    both = each(lambda t, p: _dot3_bd(jnp.concatenate([t, p], axis=0), p, bdmask), tinv, p2)
    tinv = each(lambda t, x: t + x[:C], tinv, both)
    tinv = each(lambda t, x: t + _dot3_bd(t, x[C:], bdmask), tinv, both)
    for b in (8, 16, 32):
        offd = jnp.logical_and(blk[2 * b], jnp.logical_not(blk[b]))
        y = each(lambda l, t: _dot3_bd(jnp.where(offd, l, 0.0), t, bdmask), lmat, tinv)
        tinv = each(lambda t, x: t - _dot3_bd(t, x, bdmask), tinv, y)
    t_bd = [_bd4(t.astype(BF16), bdmask) for t in tinv]

    rhs = []
    qk_out = []
    for (c, g), gcx, k, q, bcat, bh in zip(chains, gc, k_cat, q_cat, beta_cat, beta_h):
        egc = jnp.exp(gcx)
        g_last = gcx[C - 1:C, :]
        kf = k.astype(F32)
        kbe = kf * bcat * egc
        qg = q.astype(F32) * egc
        kdec = kf * jnp.exp(g_last - gcx)
        eg_ref[c, :, lanes_of(g)] = jnp.exp(g_last)
        rhs_rows = []
        qk_rows = []
        for hh in range(DN_GROUP):
            h = g * DN_GROUP + hh
            slab = slice((hh // 2) * LANES, (hh // 2 + 1) * LANES)
            sel = sels[hh % 2]
            vb = v_ref[rows_of(c), h * DN_VAL_DIM:(h + 1) * DN_VAL_DIM].astype(F32) * bh[hh]
            rhs_rows.append(jnp.concatenate([vb, jnp.where(sel, kbe[:, slab], 0.0)], axis=1))
            qk_rows.append(jnp.concatenate([jnp.where(sel, qg[:, slab], 0.0),
                                            jnp.where(sel, kdec[:, slab], 0.0)], axis=1))
        rhs.append(jnp.concatenate(rhs_rows, axis=0).astype(BF16))
        qk_out.append(jnp.concatenate(qk_rows, axis=0))
    uw = each(_dot, t_bd, rhs)
    for (c, g), x, qkx in zip(chains, uw, qk_out):
        rows_g = slice(g * DN_GROUP * C, (g + 1) * DN_GROUP * C)
        u_ref[c, rows_g, :] = x[:, :DN_VAL_DIM]
        wqk_ref[c, rows_g, 0:LANES] = x[:, DN_VAL_DIM:].astype(wqk_ref.dtype)
        wqk_ref[c, rows_g, LANES:] = qkx.astype(wqk_ref.dtype)


def _dnlocal(qn, kn, v, bg, batch, seq):
    ntot = batch * seq // DN_CHUNK
    nsteps = ntot // DNL_CHUNKS
    rows = DNL_CHUNKS * DN_CHUNK
    row = lambda i: (i, 0)
    blk3 = lambda i: (i, 0, 0)
    return pl.pallas_call(
        _dnlocal_kernel,
        grid=(nsteps,),
        in_specs=[
            pl.BlockSpec((rows, DN_QK_WIDTH), row),
            pl.BlockSpec((rows, DN_QK_WIDTH), row),
            pl.BlockSpec((rows, DN_V_WIDTH), row),
            pl.BlockSpec((rows, LANES), row),
        ],
        out_specs=[
            pl.BlockSpec((DNL_CHUNKS, DN_ROWS, DN_VAL_DIM), blk3),
            pl.BlockSpec((DNL_CHUNKS, DN_ROWS, 3 * LANES), blk3),
            pl.BlockSpec((rows, DN_QK_WIDTH), row),
            pl.BlockSpec((DNL_CHUNKS, 1, DN_QK_WIDTH), blk3),
        ],
        out_shape=[
            jax.ShapeDtypeStruct((ntot, DN_ROWS, DN_VAL_DIM), F32),
            jax.ShapeDtypeStruct((ntot, DN_ROWS, 3 * LANES), BF16),
            jax.ShapeDtypeStruct((ntot * DN_CHUNK, DN_QK_WIDTH), BF16),
            jax.ShapeDtypeStruct((ntot, 1, DN_QK_WIDTH), F32),
        ],
        compiler_params=pltpu.CompilerParams(
            dimension_semantics=("parallel",), vmem_limit_bytes=VMEM_LIMIT),
        name="dnlocal",
    )(qn, kn, v, bg)


def _dnscan_kernel(eg_ref, u_ref, wqk_ref, intra_ref, z_ref, nw_ref, o_ref, state_ref, *, nchunks):
    C = DN_CHUNK
    b = pl.program_id(0)
    n = pl.program_id(1)

    @pl.when(n == 0)
    def _():
        state_ref[...] = jnp.zeros_like(state_ref)

    rb = lax.broadcasted_iota(jnp.int32, (2 * C, LANES), 0) // C
    cb = lax.broadcasted_iota(jnp.int32, (2 * C, LANES), 1) // C
    bd2 = rb == cb
    nw = nw_ref[...]
    base = (b * nchunks + n) * DN_HEADS

    pairs = range(DN_HEADS // 2)
    rows = [slice(2 * p * C, (2 * p + 2) * C) for p in pairs]
    s_old = [(state_ref[2 * p], state_ref[2 * p + 1]) for p in pairs]
    s_bf = [jnp.concatenate(s, axis=0).astype(BF16) for s in s_old]
    wq = [jnp.concatenate([wqk_ref[0, r, 0:LANES], wqk_ref[0, r, LANES:2 * LANES]], axis=0) for r in rows]
    ws_qs = [_dot(a, s) for a, s in zip(wq, s_bf)]
    v_new = [(u_ref[0, r, :] - x[:2 * C]).astype(BF16) for r, x in zip(rows, ws_qs)]
    ibd = []
    for p in pairs:
        islab = intra_ref[:, p * LANES:(p + 1) * LANES]
        ibd.append(jnp.where(bd2, jnp.concatenate([islab, islab], axis=0), jnp.zeros((), islab.dtype)))
    o = [x[2 * C:] + _dot(a, v) for x, a, v in zip(ws_qs, ibd, v_new)]
    kv = [_dot_tn(wqk_ref[0, r, 2 * LANES:], v) for r, v in zip(rows, v_new)]
    for p in pairs:
        state_ref[2 * p] = s_old[p][0] * eg_ref[base + 2 * p] + kv[p][:C]
        state_ref[2 * p + 1] = s_old[p][1] * eg_ref[base + 2 * p + 1] + kv[p][C:]
    for h in range(DN_HEADS):
        oh = o[h // 2][(h % 2) * C:(h % 2 + 1) * C]
        ms = jnp.mean(oh * oh, axis=-1, keepdims=True)
        zh = z_ref[:, h * DN_VAL_DIM:(h + 1) * DN_VAL_DIM].astype(F32)
        on = oh * lax.rsqrt(ms + NORM_EPS) * nw * (zh * _sigmoid(zh))
        o_ref[:, h * DN_VAL_DIM:(h + 1) * DN_VAL_DIM] = on.astype(o_ref.dtype)


def _dnscan(eg_flat, u, wqk, intra, proj, dn_norm_w, batch, seq):
    nchunks = seq // DN_CHUNK
    t = batch * seq
    zcol = COL_Z // DN_V_WIDTH
    blk3 = lambda b, n: (b * nchunks + n, 0, 0)
    row = lambda b, n: (b * nchunks + n, 0)
    return pl.pallas_call(
        functools.partial(_dnscan_kernel, nchunks=nchunks),
        grid=(batch, nchunks),
        in_specs=[
            pl.BlockSpec(memory_space=pltpu.SMEM),
            pl.BlockSpec((1, DN_ROWS, DN_VAL_DIM), blk3),
            pl.BlockSpec((1, DN_ROWS, 3 * LANES), blk3),
            pl.BlockSpec((DN_CHUNK, DN_QK_WIDTH), row),
            pl.BlockSpec((DN_CHUNK, DN_V_WIDTH), lambda b, n: (b * nchunks + n, zcol)),
            pl.BlockSpec((1, DN_VAL_DIM), lambda b, n: (0, 0)),
        ],
        out_specs=pl.BlockSpec((DN_CHUNK, DN_V_WIDTH), row),
        out_shape=jax.ShapeDtypeStruct((t, DN_V_WIDTH), BF16),
        scratch_shapes=[pltpu.VMEM((DN_HEADS, DN_KEY_DIM, DN_VAL_DIM), F32)],
        compiler_params=pltpu.CompilerParams(
            dimension_semantics=("parallel", "arbitrary"), vmem_limit_bytes=VMEM_LIMIT),
        name="dnscan",
    )(eg_flat, u, wqk, intra, proj, dn_norm_w)


MG_T = 512


def _merge_kernel(oa_ref, ob_ref, ga_ref, gb_ref, x_ref, mod_ref, wsb_ref, wdn_ref, wout_ref,
                  nw_ref, x1_ref, h2_ref):
    pa = _dot(oa_ref[...], wsb_ref[...])
    pb = _dot(ob_ref[...], wdn_ref[...])
    merged = _sigmoid(ga_ref[...].astype(F32)) * pa + _sigmoid(gb_ref[...].astype(F32)) * pb
    y = _dot(merged.astype(BF16), wout_ref[...])
    m = mod_ref[0]
    x1 = x_ref[...] + m[2:3, :] * y
    x1_ref[...] = x1
    ms = jnp.mean(x1 * x1, axis=-1, keepdims=True)
    hn = x1 * lax.rsqrt(ms + NORM_EPS) * nw_ref[...]
    h2_ref[...] = (hn * (1.0 + m[4:5, :]) + m[3:4, :]).astype(h2_ref.dtype)


def _merge(o_a, o_b, proj, x2, mod3, w_sb, w_dn, w_out, norm2_w, seq):
    t = x2.shape[0]
    tiles_per_seq = seq // MG_T
    gcol = COL_GATE // D_MODEL
    const = lambda i: (0, 0)
    return pl.pallas_call(
        _merge_kernel,
        grid=(t // MG_T,),
        in_specs=[
            pl.BlockSpec((MG_T, SB_WIDTH), lambda i: (i, 0)),
            pl.BlockSpec((MG_T, DN_V_WIDTH), lambda i: (i, 0)),
            pl.BlockSpec((MG_T, D_MODEL), lambda i: (i, gcol)),
            pl.BlockSpec((MG_T, D_MODEL), lambda i: (i, gcol + 1)),
            pl.BlockSpec((MG_T, D_MODEL), lambda i: (i, 0)),
            pl.BlockSpec((1, 6, D_MODEL), lambda i: (i // tiles_per_seq, 0, 0)),
            pl.BlockSpec((SB_WIDTH, D_MODEL), const),
            pl.BlockSpec((DN_V_WIDTH, D_MODEL), const),
            pl.BlockSpec((D_MODEL, D_MODEL), const),
            pl.BlockSpec((1, D_MODEL), const),
        ],
        out_specs=[
            pl.BlockSpec((MG_T, D_MODEL), lambda i: (i, 0)),
            pl.BlockSpec((MG_T, D_MODEL), lambda i: (i, 0)),
        ],
        out_shape=[
            jax.ShapeDtypeStruct((t, D_MODEL), F32),
            jax.ShapeDtypeStruct((t, D_MODEL), BF16),
        ],
        compiler_params=pltpu.CompilerParams(
            dimension_semantics=("parallel",), vmem_limit_bytes=VMEM_LIMIT),
        name="merge",
    )(o_a, o_b, proj, proj, x2, mod3, w_sb, w_dn, w_out, norm2_w)


FI_TM = 1024
FI_TN = 256


def _ffnin_kernel(h_ref, halo_ref, wg_ref, wu_ref, cwg_ref, cwu_ref, bg_ref, bu_ref, o_ref,
                  sg_ref, su_ref, *, tiles_per_seq):
    i = pl.program_id(0)
    h = h_ref[...]
    hp = halo_ref[...]
    keep = jnp.where(i % tiles_per_seq == 0, 0.0, 1.0)

    def branch(w_ref, cw_ref, b_ref, s_ref):
        u = _dot(h, w_ref[...])
        s_ref[0:HALO, :] = _dot(hp, w_ref[...]) * keep
        s_ref[HALO:, :] = u
        cw = cw_ref[...]
        y = cw[FFN_CONV_WIDTH - 1:FFN_CONV_WIDTH, :] * u + b_ref[...]
        for d in range(1, FFN_CONV_WIDTH):
            y = y + cw[FFN_CONV_WIDTH - 1 - d:FFN_CONV_WIDTH - d, :] * s_ref[pl.ds(HALO - d, FI_TM), :]
        return y

    yg = branch(wg_ref, cwg_ref, bg_ref, sg_ref)
    yu = branch(wu_ref, cwu_ref, bu_ref, su_ref)
    o_ref[...] = (yg * _sigmoid(yg) * yu).astype(o_ref.dtype)


def _ffnin(h2, w_ffn_in, conv_w, conv_b, seq):
    t = h2.shape[0]
    tiles_per_seq = seq // FI_TM
    halo_blocks = FI_TM // HALO
    ncol = D_FF // FI_TN
    return pl.pallas_call(
        functools.partial(_ffnin_kernel, tiles_per_seq=tiles_per_seq),
        grid=(t // FI_TM, ncol),
        in_specs=[
            pl.BlockSpec((FI_TM, D_MODEL), lambda i, j: (i, 0)),
            pl.BlockSpec((HALO, D_MODEL), lambda i, j: (jnp.maximum(i * halo_blocks - 1, 0), 0)),
            pl.BlockSpec((D_MODEL, FI_TN), lambda i, j: (0, j)),
            pl.BlockSpec((D_MODEL, FI_TN), lambda i, j: (0, ncol + j)),
            pl.BlockSpec((FFN_CONV_WIDTH, FI_TN), lambda i, j: (0, j)),
            pl.BlockSpec((FFN_CONV_WIDTH, FI_TN), lambda i, j: (0, ncol + j)),
            pl.BlockSpec((1, FI_TN), lambda i, j: (0, j)),
            pl.BlockSpec((1, FI_TN), lambda i, j: (0, ncol + j)),
        ],
        out_specs=pl.BlockSpec((FI_TM, FI_TN), lambda i, j: (i, j)),
        out_shape=jax.ShapeDtypeStruct((t, D_FF), BF16),
        scratch_shapes=[pltpu.VMEM((FI_TM + HALO, FI_TN), F32),
                        pltpu.VMEM((FI_TM + HALO, FI_TN), F32)],
        compiler_params=pltpu.CompilerParams(
            dimension_semantics=("parallel", "arbitrary"), vmem_limit_bytes=VMEM_LIMIT),
        name="ffnin",
    )(h2, h2, w_ffn_in, w_ffn_in, conv_w, conv_w, conv_b, conv_b)


FO_T = 512


def _ffnout_kernel(a_ref, w_ref, x1_ref, mod_ref, nw_ref, o_ref):
    y = _dot(a_ref[...], w_ref[...])
    m = mod_ref[0]
    x2 = x1_ref[...] + m[5:6, :] * y
    ms = jnp.mean(x2 * x2, axis=-1, keepdims=True)
    o_ref[...] = x2 * lax.rsqrt(ms + NORM_EPS) * nw_ref[...]


def _ffnout(act, w_ffn_out, x1, mod3, final_w, seq):
    t = x1.shape[0]
    tiles_per_seq = seq // FO_T
    return pl.pallas_call(
        _ffnout_kernel,
        grid=(t // FO_T,),
        in_specs=[
            pl.BlockSpec((FO_T, D_FF), lambda i: (i, 0)),
            pl.BlockSpec((D_FF, D_MODEL), lambda i: (0, 0)),
            pl.BlockSpec((FO_T, D_MODEL), lambda i: (i, 0)),
            pl.BlockSpec((1, 6, D_MODEL), lambda i: (i // tiles_per_seq, 0, 0)),
            pl.BlockSpec((1, D_MODEL), lambda i: (0, 0)),
        ],
        out_specs=pl.BlockSpec((FO_T, D_MODEL), lambda i: (i, 0)),
        out_shape=jax.ShapeDtypeStruct((t, D_MODEL), F32),
        compiler_params=pltpu.CompilerParams(
            dimension_semantics=("parallel",), vmem_limit_bytes=VMEM_LIMIT),
        name="ffnout",
    )(act, w_ffn_out, x1, mod3, final_w)


def kernel(x, c, w_ada, b_ada, norm1_w, w_in, dn_conv_w, dn_A_log, dn_dt_bias, dn_norm_w,
           w_proj_sb, w_proj_dn, w_out, norm2_w, w_ffn_in, ffn_conv_w, ffn_conv_b, w_ffn_out,
           final_norm_w):
    batch, seq, d = x.shape
    depth = w_ada.shape[0]
    assert depth == 1, "the final rmsnorm is fused into the last layer's FFN-out kernel"
    t = batch * seq
    xt = x.reshape(t, d)
    c_pad = jnp.pad(c, ((0, 8 - batch), (0, 0)))

    sb_end = 3 * SB_WIDTH
    dn_end = sb_end + DN_CONV_CH
    z_end = dn_end + DN_V_WIDTH
    tail_end = z_end + 2 * DN_HEADS

    for l in range(depth):
        wl = w_in[l]
        w_main = jnp.concatenate(
            [wl[:, sb_end:dn_end], wl[:, dn_end:z_end], wl[:, tail_end:], wl[:, :sb_end]],
            axis=1).astype(BF16)
        w_tail = jnp.pad(wl[:, z_end:tail_end], ((0, 0), (0, LANES - 2 * DN_HEADS))).astype(BF16)
        alog_pad = jnp.pad(dn_A_log[l], (DN_HEADS, LANES - 2 * DN_HEADS)).reshape(1, LANES)
        dtb_pad = jnp.pad(dn_dt_bias[l], (DN_HEADS, LANES - 2 * DN_HEADS)).reshape(1, LANES)

        mod = _ada(c_pad, w_ada[l], b_ada[l].reshape(1, -1))
        mod3 = mod[:batch].reshape(batch, 6, d)

        proj, tail = _inproj(xt, mod3, norm1_w[l].reshape(1, d), w_main, w_tail, seq)
        o_a = _sb_attention(proj, batch, seq)
        qn, kn, vv, bg = _dnprep(proj, tail, dn_conv_w[l], alog_pad, dtb_pad, seq)
        u, wqk, intra, eg = _dnlocal(qn, kn, vv, bg, batch, seq)
        eg_flat = eg[:, 0, ::DN_KEY_DIM].reshape(-1)
        o_b = _dnscan(eg_flat, u, wqk, intra, proj, dn_norm_w[l].reshape(1, -1), batch, seq)
        x1, h2 = _merge(o_a, o_b, proj, xt, mod3, w_proj_sb[l].astype(BF16),
                        w_proj_dn[l].astype(BF16), w_out[l].astype(BF16),
                        norm2_w[l].reshape(1, d), seq)
        act = _ffnin(h2, w_ffn_in[l].astype(BF16), ffn_conv_w[l], ffn_conv_b[l].reshape(1, -1), seq)
        xt = _ffnout(act, w_ffn_out[l].astype(BF16), x1, mod3, final_norm_w.reshape(1, d), seq)
    return xt.reshape(batch, seq, d)
```

```python
import functools

import jax
import jax.numpy as jnp
from jax import lax
from jax.experimental import pallas as pl
from jax.experimental.pallas import tpu as pltpu

F32 = jnp.float32
BF16 = jnp.bfloat16

D_MODEL = 1024
SB_HEADS = 8
SB_HEAD_DIM = 64
SB_WIDTH = SB_HEADS * SB_HEAD_DIM
DN_HEADS = 8
DN_KEY_DIM = 64
DN_VAL_DIM = 128
DN_QK_WIDTH = DN_HEADS * DN_KEY_DIM
DN_V_WIDTH = DN_HEADS * DN_VAL_DIM
DN_CONV_CH = 2 * DN_QK_WIDTH + DN_V_WIDTH
DN_CONV_WIDTH = 4
DN_CHUNK = 64
D_FF = 2816
FFN_CONV_WIDTH = 3
NORM_EPS = 1e-6
L2_EPS = 1e-6

LANES = 128
HALO = 16

COL_DN = 0
COL_Z = DN_CONV_CH
COL_GATE = COL_Z + DN_V_WIDTH
COL_SB = COL_GATE + 2 * D_MODEL
PROJ_WIDTH = COL_SB + 3 * SB_WIDTH

VMEM_LIMIT = 56 * 1024 * 1024


def _dot(a, b):
    return jnp.dot(a, b, preferred_element_type=F32)


def _dot_nt(a, b):
    return lax.dot_general(a, b, (((1,), (1,)), ((), ())), preferred_element_type=F32)


def _dot_tn(a, b):
    return lax.dot_general(a, b, (((0,), (0,)), ((), ())), preferred_element_type=F32)


def _split2(a):
    hi = a.astype(BF16)
    lo = (a - hi.astype(F32)).astype(BF16)
    return hi, lo


def _split3(a):
    a1 = a.astype(BF16)
    r = a - a1.astype(F32)
    a2 = r.astype(BF16)
    a3 = (r - a2.astype(F32)).astype(BF16)
    return a1, a2, a3


def _dot_exact_lhs(m, b):
    b1, b2, b3 = _split3(b)
    return _dot(m, b1) + (_dot(m, b2) + _dot(m, b3))


def _dot_exact_rhs2(a, m):
    hi, lo = _split2(a)
    return _dot(hi, m) + _dot(lo, m)


def _dot3(a, b):
    a1, a2 = _split2(a)
    b1, b2 = _split2(b)
    return _dot(a1, b1) + (_dot(a1, b2) + _dot(a2, b1))


def _sigmoid(x):
    return 1.0 / (1.0 + jnp.exp(-x))


def _softplus(x):
    return jnp.maximum(x, 0.0) + jnp.log(1.0 + jnp.exp(-jnp.abs(x)))


def _ada_kernel(c_ref, w_ref, b_ref, o_ref):
    c = c_ref[...]
    ca = c * _sigmoid(c)
    o_ref[...] = _dot3(ca, w_ref[...]) + b_ref[...]


def _ada(c_pad, w_ada, b_ada):
    n = w_ada.shape[1]
    tn = 1536
    return pl.pallas_call(
        _ada_kernel,
        grid=(n // tn,),
        in_specs=[
            pl.BlockSpec((8, D_MODEL), lambda j: (0, 0)),
            pl.BlockSpec((D_MODEL, tn), lambda j: (0, j)),
            pl.BlockSpec((1, tn), lambda j: (0, j)),
        ],
        out_specs=pl.BlockSpec((8, tn), lambda j: (0, j)),
        out_shape=jax.ShapeDtypeStruct((8, n), F32),
        compiler_params=pltpu.CompilerParams(
            dimension_semantics=("arbitrary",), vmem_limit_bytes=VMEM_LIMIT),
        name="ada",
    )(c_pad, w_ada, b_ada)


def _inproj_kernel(x_ref, mod_ref, nw_ref, w_ref, wt_ref, o_ref, ot_ref, h_ref):
    @pl.when(pl.program_id(1) == 0)
    def _():
        x = x_ref[...]
        ms = jnp.mean(x * x, axis=-1, keepdims=True)
        y = x * lax.rsqrt(ms + NORM_EPS) * nw_ref[...]
        m = mod_ref[0]
        h = (y * (1.0 + m[1:2, :]) + m[0:1, :]).astype(BF16)
        h_ref[...] = h
        ot_ref[...] = _dot(h, wt_ref[...])

    o_ref[...] = _dot(h_ref[...], w_ref[...]).astype(o_ref.dtype)


def _inproj(x2, mod3, norm_w, w_main, w_tail, seq):
    t = x2.shape[0]
    tm, tn = 1024, 1664
    tiles_per_seq = seq // tm
    return pl.pallas_call(
        _inproj_kernel,
        grid=(t // tm, PROJ_WIDTH // tn),
        in_specs=[
            pl.BlockSpec((tm, D_MODEL), lambda i, j: (i, 0)),
            pl.BlockSpec((1, 6, D_MODEL), lambda i, j: (i // tiles_per_seq, 0, 0)),
            pl.BlockSpec((1, D_MODEL), lambda i, j: (0, 0)),
            pl.BlockSpec((D_MODEL, tn), lambda i, j: (0, j)),
            pl.BlockSpec((D_MODEL, LANES), lambda i, j: (0, 0)),
        ],
        out_specs=[
            pl.BlockSpec((tm, tn), lambda i, j: (i, j)),
            pl.BlockSpec((tm, LANES), lambda i, j: (i, 0)),
        ],
        out_shape=[
            jax.ShapeDtypeStruct((t, PROJ_WIDTH), BF16),
            jax.ShapeDtypeStruct((t, LANES), F32),
        ],
        scratch_shapes=[pltpu.VMEM((tm, D_MODEL), BF16)],
        compiler_params=pltpu.CompilerParams(
            dimension_semantics=("parallel", "arbitrary"), vmem_limit_bytes=VMEM_LIMIT),
        name="inproj",
    )(x2, mod3, norm_w, w_main, w_tail)


SB_T = 256
SB_DEAD_LOG = -100.0


def _sb_kernel(q_ref, k_ref, v_ref, o_ref):
    i = pl.program_id(2)
    q = q_ref[...]
    lane = lax.broadcasted_iota(jnp.int32, (1, LANES), 1)
    first = lane < SB_HEAD_DIM
    qs = q * jnp.asarray(SB_HEAD_DIM ** -0.5, BF16)
    zero = jnp.zeros_like(qs)
    H = SB_T // 2
    qh = (jnp.where(first, qs, zero), jnp.where(first, zero, qs))
    q_chunks = [qh[e][s * H:(s + 1) * H] for e in range(2) for s in range(2)]
    r = lax.broadcasted_iota(jnp.int32, (SB_T, SB_T), 0)
    c = lax.broadcasted_iota(jnp.int32, (SB_T, SB_T), 1)
    upper = jnp.where(r > c, 1.0, 0.0).astype(BF16)
    rh = lax.broadcasted_iota(jnp.int32, (H, SB_T), 0)
    ch = lax.broadcasted_iota(jnp.int32, (H, SB_T), 1)
    mask_lo = (ch < rh)[:, :H]
    mask_hi = ch < rh + H

    def block(kbs, vbs, ups, masks, carries):
        z = [_dot_nt(qc, kb) for qc, kb in zip(q_chunks, kbs)]
        sp = [_softplus(x) for x in z]
        l1 = [-x if m is None else jnp.where(m, -x, 0.0) for x, m in zip(sp, masks)]
        cs = [_dot_exact_rhs2(x, up) for x, up in zip(l1, ups)]
        a = [jnp.exp((x - s) + y + cr) for x, s, y, cr in zip(z, sp, cs, carries)]
        a = [x if m is None else jnp.where(m, x, 0.0) for x, m in zip(a, masks)]
        o = [_dot(x.astype(BF16), vb) for x, vb in zip(a, vbs)]
        carries = [cr + jnp.sum(x, axis=-1, keepdims=True) for cr, x in zip(carries, l1)]
        return o, carries

    off_d = pl.multiple_of(i * SB_T, SB_T)
    k_d = k_ref[pl.ds(off_d, SB_T), :]
    v_d = v_ref[pl.ds(off_d, SB_T), :]
    halves = (0, 1, 0, 1)
    o, carry = block([k_d if s else k_d[:H] for s in halves],
                     [v_d if s else v_d[:H] for s in halves],
                     [upper if s else upper[:H, :H] for s in halves],
                     [mask_hi if s else mask_lo for s in halves],
                     [jnp.zeros((H, 1), F32)] * 4)

    def live(carries):
        m = jnp.maximum(jnp.maximum(carries[0], carries[1]), jnp.maximum(carries[2], carries[3]))
        return jnp.max(m) > SB_DEAD_LOG

    def cond(st):
        n, go = st[0], st[1]
        return jnp.logical_and(n < i, go)

    def body(st):
        n, _, o, carry = st
        off = pl.multiple_of((i - 1 - n) * SB_T, SB_T)
        kb = k_ref[pl.ds(off, SB_T), :]
        vb = v_ref[pl.ds(off, SB_T), :]
        a, carry = block([kb] * 4, [vb] * 4, [upper] * 4, [None] * 4, carry)
        return n + 1, live(carry), [x + y for x, y in zip(o, a)], carry

    st = lax.while_loop(cond, body, (jnp.int32(0), live(carry), o, carry))
    o = st[2]
    o_ref[0:H, :] = jnp.where(first, o[0], o[2]).astype(o_ref.dtype)
    o_ref[H:, :] = jnp.where(first, o[1], o[3]).astype(o_ref.dtype)


def _sb_attention(proj, batch, seq):
    t = proj.shape[0]
    nq = seq // SB_T
    pairs = SB_HEADS // 2
    qcol = COL_SB // LANES
    kcol = qcol + SB_WIDTH // LANES
    vcol = kcol + SB_WIDTH // LANES
    return pl.pallas_call(
        _sb_kernel,
        grid=(batch, pairs, nq),
        in_specs=[
            pl.BlockSpec((SB_T, LANES), lambda b, p, i: (b * nq + i, qcol + p)),
            pl.BlockSpec((seq, LANES), lambda b, p, i: (b, kcol + p)),
            pl.BlockSpec((seq, LANES), lambda b, p, i: (b, vcol + p)),
        ],
        out_specs=pl.BlockSpec((SB_T, LANES), lambda b, p, i: (b * nq + i, p)),
        out_shape=jax.ShapeDtypeStruct((t, SB_WIDTH), BF16),
        compiler_params=pltpu.CompilerParams(
            dimension_semantics=("parallel", "parallel", "arbitrary"),
            vmem_limit_bytes=VMEM_LIMIT),
        name="sb",
    )(proj, proj, proj)


DNP_T = 512


def _dnprep_kernel(cur_ref, halo_ref, tail_ref, cw_ref, alog_ref, dtb_ref,
                   q_ref, k_ref, v_ref, bg_ref, xs_ref, *, tiles_per_seq):
    i = pl.program_id(0)
    cur = cur_ref[...].astype(F32)
    prev = halo_ref[...].astype(F32)
    prev = jnp.where(i % tiles_per_seq == 0, 0.0, prev)
    xs_ref[0:HALO, :] = prev
    xs_ref[HALO:, :] = cur
    w = cw_ref[...]
    y = w[DN_CONV_WIDTH - 1:DN_CONV_WIDTH, :] * cur
    for d in range(1, DN_CONV_WIDTH):
        y = y + w[DN_CONV_WIDTH - 1 - d:DN_CONV_WIDTH - d, :] * xs_ref[pl.ds(HALO - d, DNP_T), :]
    s = y * _sigmoid(y)

    r = lax.broadcasted_iota(jnp.int32, (DN_QK_WIDTH, DN_QK_WIDTH), 0) // DN_KEY_DIM
    c = lax.broadcasted_iota(jnp.int32, (DN_QK_WIDTH, DN_QK_WIDTH), 1) // DN_KEY_DIM
    same_head = jnp.where(r == c, 1.0, 0.0).astype(BF16)

    def l2n(t):
        ss = _dot_exact_rhs2(t * t, same_head)
        return t * lax.rsqrt(ss + L2_EPS)

    q = s[:, 0:DN_QK_WIDTH]
    k = s[:, DN_QK_WIDTH:2 * DN_QK_WIDTH]
    q_ref[...] = (l2n(q) * (DN_KEY_DIM ** -0.5)).astype(q_ref.dtype)
    k_ref[...] = l2n(k).astype(k_ref.dtype)
    v_ref[...] = s[:, 2 * DN_QK_WIDTH:].astype(v_ref.dtype)

    tl = tail_ref[...]
    lane = lax.broadcasted_iota(jnp.int32, (1, LANES), 1)
    beta = _sigmoid(tl)
    g = -jnp.exp(alog_ref[...]) * _softplus(tl + dtb_ref[...])
    bg_ref[...] = jnp.where(lane < DN_HEADS, beta, g)


def _dnprep(proj, tail, conv_w, alog_pad, dtb_pad, seq):
    t = proj.shape[0]
    tiles_per_seq = seq // DNP_T
    halo_blocks = DNP_T // HALO
    return pl.pallas_call(
        functools.partial(_dnprep_kernel, tiles_per_seq=tiles_per_seq),
        grid=(t // DNP_T,),
        in_specs=[
            pl.BlockSpec((DNP_T, DN_CONV_CH), lambda i: (i, 0)),
            pl.BlockSpec((HALO, DN_CONV_CH), lambda i: (jnp.maximum(i * halo_blocks - 1, 0), 0)),
            pl.BlockSpec((DNP_T, LANES), lambda i: (i, 0)),
            pl.BlockSpec((DN_CONV_WIDTH, DN_CONV_CH), lambda i: (0, 0)),
            pl.BlockSpec((1, LANES), lambda i: (0, 0)),
            pl.BlockSpec((1, LANES), lambda i: (0, 0)),
        ],
        out_specs=[
            pl.BlockSpec((DNP_T, DN_QK_WIDTH), lambda i: (i, 0)),
            pl.BlockSpec((DNP_T, DN_QK_WIDTH), lambda i: (i, 0)),
            pl.BlockSpec((DNP_T, DN_V_WIDTH), lambda i: (i, 0)),
            pl.BlockSpec((DNP_T, LANES), lambda i: (i, 0)),
        ],
        out_shape=[
            jax.ShapeDtypeStruct((t, DN_QK_WIDTH), BF16),
            jax.ShapeDtypeStruct((t, DN_QK_WIDTH), BF16),
            jax.ShapeDtypeStruct((t, DN_V_WIDTH), BF16),
            jax.ShapeDtypeStruct((t, LANES), F32),
        ],
        scratch_shapes=[pltpu.VMEM((DNP_T + HALO, DN_CONV_CH), F32)],
        compiler_params=pltpu.CompilerParams(
            dimension_semantics=("parallel",), vmem_limit_bytes=VMEM_LIMIT),
        name="dnprep",
    )(proj, proj, tail, conv_w, alog_pad, dtb_pad)


DN_GROUP = 4
DN_GROUP_LANES = DN_GROUP * DN_KEY_DIM
DN_ROWS = DN_HEADS * DN_CHUNK
DNL_CHUNKS = 4


def _bd4(x, mask):
    return jnp.where(mask, jnp.concatenate([x, x, x, x], axis=0), jnp.zeros((), x.dtype))


def _dot3_bd(a, b, mask):
    a1, a2 = _split2(a)
    b1, b2 = _split2(b)
    bd1 = _bd4(b1, mask)
    bd2 = _bd4(b2, mask)
    return _dot(a1, bd1) + (_dot(a1, bd2) + _dot(a2, bd1))


def _dot1_bd(a, b, mask):
    return _dot(a.astype(BF16), _bd4(b.astype(BF16), mask))


def _dnlocal_kernel(q_ref, k_ref, v_ref, bg_ref, u_ref, wqk_ref, intra_ref, eg_ref):
    C = DN_CHUNK
    GL = DN_GROUP_LANES
    ri = lax.broadcasted_iota(jnp.int32, (C, GL), 0)
    cj = lax.broadcasted_iota(jnp.int32, (C, GL), 1) % C
    incl = ri >= cj
    strict = ri > cj
    after = jnp.where(strict, 1.0, 0.0)
    eye = jnp.where(ri == cj, 1.0, 0.0)
    blk = {b: (ri // b) == (cj // b) for b in (8, 16, 32, 64)}
    r2 = lax.broadcasted_iota(jnp.int32, (C, C), 0)
    c2 = lax.broadcasted_iota(jnp.int32, (C, C), 1)
    lower_incl = jnp.where(r2 >= c2, 1.0, 0.0).astype(BF16)
    rb = lax.broadcasted_iota(jnp.int32, (GL, GL), 0) // C
    cb = lax.broadcasted_iota(jnp.int32, (GL, GL), 1) // C
    bdmask = rb == cb
    lane = lax.broadcasted_iota(jnp.int32, (1, LANES), 1)
    first = lane < DN_KEY_DIM
    sels = (first, jnp.logical_not(first))

    chains = [(c, g) for c in range(DNL_CHUNKS) for g in range(DN_HEADS // DN_GROUP)]
    rows_of = lambda c: slice(c * C, (c + 1) * C)
    lanes_of = lambda g: slice(g * GL, (g + 1) * GL)
    each = lambda f, *ls: [f(*xs) for xs in zip(*ls)]

    def head_bcast(c, g, col0):
        bgc = bg_ref[rows_of(c), :]
        return [jnp.broadcast_to(bgc[:, col0 + h:col0 + h + 1], (C, LANES))
                for h in range(g * DN_GROUP, (g + 1) * DN_GROUP)]

    def cat(hs):
        return jnp.concatenate([jnp.where(first, hs[0], hs[1]), jnp.where(first, hs[2], hs[3])], axis=1)

    beta_h = [head_bcast(c, g, 0) for c, g in chains]
    beta_cat = each(cat, beta_h)
    g_cat = [cat(head_bcast(c, g, DN_HEADS)) for c, g in chains]
    k_cat = [k_ref[rows_of(c), lanes_of(g)] for c, g in chains]
    q_cat = [q_ref[rows_of(c), lanes_of(g)] for c, g in chains]

    gc = [_dot_exact_lhs(lower_incl, x) for x in g_cat]
    diff = [_dot_exact_lhs(lower_incl, x * after) for x in g_cat]
    kq = each(lambda k, q: _dot_nt(jnp.concatenate([k, q], axis=0), _bd4(k, bdmask)), k_cat, q_cat)
    decay = [jnp.where(incl, jnp.exp(x), 0.0) for x in diff]
    lmat = each(lambda x, b, d: jnp.where(strict, x[:C] * b * d, 0.0), kq, beta_cat, decay)
    for (c, g), x, d in zip(chains, kq, decay):
        intra_ref[rows_of(c), lanes_of(g)] = jnp.where(incl, x[C:] * d, 0.0).astype(intra_ref.dtype)

    pw = [jnp.where(blk[8], -x, 0.0) for x in lmat]
    tinv = [eye + x for x in pw]
    p2 = [_dot3_bd(x, x, bdmask) for x in pw]
    both = each(lambda t, p: _dot3_bd(jnp.concatenate([t, p], axis=0), p, bdmask), tinv, p2)
    tinv = each(lambda t, x: t + x[:C], tinv, both)
    tinv = each(lambda t, x: t + _dot3_bd(t, x[C:], bdmask), tinv, both)
    for b in (8, 16, 32):
        offd = jnp.logical_and(blk[2 * b], jnp.logical_not(blk[b]))
        y = each(lambda l, t: _dot1_bd(jnp.where(offd, l, 0.0), t, bdmask), lmat, tinv)
        tinv = each(lambda t, x: t - _dot1_bd(t, x, bdmask), tinv, y)
    t_bd = [_bd4(t.astype(BF16), bdmask) for t in tinv]

    rhs = []
    qk_out = []
    for (c, g), gcx, k, q, bcat, bh in zip(chains, gc, k_cat, q_cat, beta_cat, beta_h):
        egc = jnp.exp(gcx)
        g_last = gcx[C - 1:C, :]
        kf = k.astype(F32)
        kbe = kf * bcat * egc
        qg = q.astype(F32) * egc
        kdec = kf * jnp.exp(g_last - gcx)
        eg_ref[c, :, lanes_of(g)] = jnp.exp(g_last)
        rhs_rows = []
        qk_rows = []
        for hh in range(DN_GROUP):
            h = g * DN_GROUP + hh
            slab = slice((hh // 2) * LANES, (hh // 2 + 1) * LANES)
            sel = sels[hh % 2]
            vb = v_ref[rows_of(c), h * DN_VAL_DIM:(h + 1) * DN_VAL_DIM].astype(F32) * bh[hh]
            rhs_rows.append(jnp.concatenate([vb, jnp.where(sel, kbe[:, slab], 0.0)], axis=1))
            qk_rows.append(jnp.concatenate([jnp.where(sel, qg[:, slab], 0.0),
                                            jnp.where(sel, kdec[:, slab], 0.0)], axis=1))
        rhs.append(jnp.concatenate(rhs_rows, axis=0).astype(BF16))
        qk_out.append(jnp.concatenate(qk_rows, axis=0))
    uw = each(_dot, t_bd, rhs)
    for (c, g), x, qkx in zip(chains, uw, qk_out):
        rows_g = slice(g * DN_GROUP * C, (g + 1) * DN_GROUP * C)
        u_ref[c, rows_g, :] = x[:, :DN_VAL_DIM]
        wqk_ref[c, rows_g, 0:LANES] = x[:, DN_VAL_DIM:].astype(wqk_ref.dtype)
        wqk_ref[c, rows_g, LANES:] = qkx.astype(wqk_ref.dtype)


def _dnlocal(qn, kn, v, bg, batch, seq):
    ntot = batch * seq // DN_CHUNK
    nsteps = ntot // DNL_CHUNKS
    rows = DNL_CHUNKS * DN_CHUNK
    row = lambda i: (i, 0)
    blk3 = lambda i: (i, 0, 0)
    return pl.pallas_call(
        _dnlocal_kernel,
        grid=(nsteps,),
        in_specs=[
            pl.BlockSpec((rows, DN_QK_WIDTH), row),
            pl.BlockSpec((rows, DN_QK_WIDTH), row),
            pl.BlockSpec((rows, DN_V_WIDTH), row),
            pl.BlockSpec((rows, LANES), row),
        ],
        out_specs=[
            pl.BlockSpec((DNL_CHUNKS, DN_ROWS, DN_VAL_DIM), blk3),
            pl.BlockSpec((DNL_CHUNKS, DN_ROWS, 3 * LANES), blk3),
            pl.BlockSpec((rows, DN_QK_WIDTH), row),
            pl.BlockSpec((DNL_CHUNKS, 1, DN_QK_WIDTH), blk3),
        ],
        out_shape=[
            jax.ShapeDtypeStruct((ntot, DN_ROWS, DN_VAL_DIM), F32),
            jax.ShapeDtypeStruct((ntot, DN_ROWS, 3 * LANES), BF16),
            jax.ShapeDtypeStruct((ntot * DN_CHUNK, DN_QK_WIDTH), BF16),
            jax.ShapeDtypeStruct((ntot, 1, DN_QK_WIDTH), F32),
        ],
        compiler_params=pltpu.CompilerParams(
            dimension_semantics=("parallel",), vmem_limit_bytes=VMEM_LIMIT),
        name="dnlocal",
    )(qn, kn, v, bg)


def _dnscan_kernel(eg_ref, u_ref, wqk_ref, intra_ref, z_ref, nw_ref, o_ref, state_ref, *,
                   batch, nchunks):
    C = DN_CHUNK
    n = pl.program_id(0)

    @pl.when(n == 0)
    def _():
        state_ref[...] = jnp.zeros_like(state_ref)

    rb = lax.broadcasted_iota(jnp.int32, (2 * C, LANES), 0) // C
    cb = lax.broadcasted_iota(jnp.int32, (2 * C, LANES), 1) // C
    bd2 = rb == cb
    nw = nw_ref[...]

    chains = [(b, p) for b in range(batch) for p in range(DN_HEADS // 2)]
    rows = lambda p: slice(2 * p * C, (2 * p + 2) * C)
    sidx = lambda b, p, e: b * DN_HEADS + 2 * p + e
    s_old = [(state_ref[sidx(b, p, 0)], state_ref[sidx(b, p, 1)]) for b, p in chains]
    s_bf = [jnp.concatenate(s, axis=0).astype(BF16) for s in s_old]
    wq = [jnp.concatenate([wqk_ref[b, 0, rows(p), 0:LANES], wqk_ref[b, 0, rows(p), LANES:2 * LANES]], axis=0)
          for b, p in chains]
    ws_qs = [_dot(a, s) for a, s in zip(wq, s_bf)]
    v_new = [(u_ref[b, 0, rows(p), :] - x[:2 * C]).astype(BF16) for (b, p), x in zip(chains, ws_qs)]
    ibd = []
    for b, p in chains:
        islab = intra_ref[b, :, p * LANES:(p + 1) * LANES]
        ibd.append(jnp.where(bd2, jnp.concatenate([islab, islab], axis=0), jnp.zeros((), islab.dtype)))
    o = [x[2 * C:] + _dot(a, v) for x, a, v in zip(ws_qs, ibd, v_new)]
    kv = [_dot_tn(wqk_ref[b, 0, rows(p), 2 * LANES:], v) for (b, p), v in zip(chains, v_new)]
    for (b, p), s, x in zip(chains, s_old, kv):
        base = (b * nchunks + n) * DN_HEADS + 2 * p
        state_ref[sidx(b, p, 0)] = s[0] * eg_ref[base] + x[:C]
        state_ref[sidx(b, p, 1)] = s[1] * eg_ref[base + 1] + x[C:]
    for (b, p), x in zip(chains, o):
        for e in range(2):
            h = 2 * p + e
            oh = x[e * C:(e + 1) * C]
            ms = jnp.mean(oh * oh, axis=-1, keepdims=True)
            zh = z_ref[b, :, h * DN_VAL_DIM:(h + 1) * DN_VAL_DIM].astype(F32)
            on = oh * lax.rsqrt(ms + NORM_EPS) * nw * (zh * _sigmoid(zh))
            o_ref[b, :, h * DN_VAL_DIM:(h + 1) * DN_VAL_DIM] = on.astype(o_ref.dtype)


def _dnscan(eg_flat, u, wqk, intra, proj, dn_norm_w, batch, seq):
    nchunks = seq // DN_CHUNK
    zcol = COL_Z // DN_V_WIDTH
    blk4 = lambda n: (0, n, 0, 0)
    out = pl.pallas_call(
        functools.partial(_dnscan_kernel, batch=batch, nchunks=nchunks),
        grid=(nchunks,),
        in_specs=[
            pl.BlockSpec(memory_space=pltpu.SMEM),
            pl.BlockSpec((batch, 1, DN_ROWS, DN_VAL_DIM), blk4),
            pl.BlockSpec((batch, 1, DN_ROWS, 3 * LANES), blk4),
            pl.BlockSpec((batch, DN_CHUNK, DN_QK_WIDTH), lambda n: (0, n, 0)),
            pl.BlockSpec((batch, DN_CHUNK, DN_V_WIDTH), lambda n: (0, n, zcol)),
            pl.BlockSpec((1, DN_VAL_DIM), lambda n: (0, 0)),
        ],
        out_specs=pl.BlockSpec((batch, DN_CHUNK, DN_V_WIDTH), lambda n: (0, n, 0)),
        out_shape=jax.ShapeDtypeStruct((batch, seq, DN_V_WIDTH), BF16),
        scratch_shapes=[pltpu.VMEM((batch * DN_HEADS, DN_KEY_DIM, DN_VAL_DIM), F32)],
        compiler_params=pltpu.CompilerParams(
            dimension_semantics=("arbitrary",), vmem_limit_bytes=VMEM_LIMIT),
        name="dnscan",
    )(eg_flat,
      u.reshape(batch, nchunks, DN_ROWS, DN_VAL_DIM),
      wqk.reshape(batch, nchunks, DN_ROWS, 3 * LANES),
      intra.reshape(batch, seq, DN_QK_WIDTH),
      proj.reshape(batch, seq, PROJ_WIDTH),
      dn_norm_w)
    return out.reshape(batch * seq, DN_V_WIDTH)


MG_T = 512


def _merge_kernel(oa_ref, ob_ref, ga_ref, gb_ref, x_ref, mod_ref, wsb_ref, wdn_ref, wout_ref,
                  nw_ref, x1_ref, h2_ref):
    pa = _dot(oa_ref[...], wsb_ref[...])
    pb = _dot(ob_ref[...], wdn_ref[...])
    merged = _sigmoid(ga_ref[...].astype(F32)) * pa + _sigmoid(gb_ref[...].astype(F32)) * pb
    y = _dot(merged.astype(BF16), wout_ref[...])
    m = mod_ref[0]
    x1 = x_ref[...] + m[2:3, :] * y
    x1_ref[...] = x1
    ms = jnp.mean(x1 * x1, axis=-1, keepdims=True)
    hn = x1 * lax.rsqrt(ms + NORM_EPS) * nw_ref[...]
    h2_ref[...] = (hn * (1.0 + m[4:5, :]) + m[3:4, :]).astype(h2_ref.dtype)


def _merge(o_a, o_b, proj, x2, mod3, w_sb, w_dn, w_out, norm2_w, seq):
    t = x2.shape[0]
    tiles_per_seq = seq // MG_T
    gcol = COL_GATE // D_MODEL
    const = lambda i: (0, 0)
    return pl.pallas_call(
        _merge_kernel,
        grid=(t // MG_T,),
        in_specs=[
            pl.BlockSpec((MG_T, SB_WIDTH), lambda i: (i, 0)),
            pl.BlockSpec((MG_T, DN_V_WIDTH), lambda i: (i, 0)),
            pl.BlockSpec((MG_T, D_MODEL), lambda i: (i, gcol)),
            pl.BlockSpec((MG_T, D_MODEL), lambda i: (i, gcol + 1)),
            pl.BlockSpec((MG_T, D_MODEL), lambda i: (i, 0)),
            pl.BlockSpec((1, 6, D_MODEL), lambda i: (i // tiles_per_seq, 0, 0)),
            pl.BlockSpec((SB_WIDTH, D_MODEL), const),
            pl.BlockSpec((DN_V_WIDTH, D_MODEL), const),
            pl.BlockSpec((D_MODEL, D_MODEL), const),
            pl.BlockSpec((1, D_MODEL), const),
        ],
        out_specs=[
            pl.BlockSpec((MG_T, D_MODEL), lambda i: (i, 0)),
            pl.BlockSpec((MG_T, D_MODEL), lambda i: (i, 0)),
        ],
        out_shape=[
            jax.ShapeDtypeStruct((t, D_MODEL), F32),
            jax.ShapeDtypeStruct((t, D_MODEL), BF16),
        ],
        compiler_params=pltpu.CompilerParams(
            dimension_semantics=("parallel",), vmem_limit_bytes=VMEM_LIMIT),
        name="merge",
    )(o_a, o_b, proj, proj, x2, mod3, w_sb, w_dn, w_out, norm2_w)


FI_TM = 512
FI_TN = 256


def _ffnin_kernel(h_ref, halo_ref, w_ref, cw_ref, b_ref, o_ref, s_ref, *, tiles_per_seq):
    i = pl.program_id(0)
    h = h_ref[...]
    hp = halo_ref[...]
    keep = jnp.where(i % tiles_per_seq == 0, 0.0, 1.0)
    nsub = D_FF // FI_TN

    def cols(br, j):
        return slice(br * D_FF + j * FI_TN, br * D_FF + (j + 1) * FI_TN)

    def matmuls(j):
        for br in range(2):
            w = w_ref[:, cols(br, j)]
            s_ref[2 * (j % 2) + br, 0:HALO, :] = _dot(hp, w) * keep
            s_ref[2 * (j % 2) + br, HALO:, :] = _dot(h, w)

    def conv_act(j):
        ys = []
        for br in range(2):
            slot = 2 * (j % 2) + br
            cw = cw_ref[:, cols(br, j)]
            y = b_ref[:, cols(br, j)]
            for d in range(FFN_CONV_WIDTH):
                y = y + cw[FFN_CONV_WIDTH - 1 - d:FFN_CONV_WIDTH - d, :] * s_ref[slot, pl.ds(HALO - d, FI_TM), :]
            ys.append(y)
        o_ref[:, j * FI_TN:(j + 1) * FI_TN] = (ys[0] * _sigmoid(ys[0]) * ys[1]).astype(o_ref.dtype)

    matmuls(0)
    for j in range(nsub):
        if j + 1 < nsub:
            matmuls(j + 1)
        conv_act(j)


def _ffnin(h2, w_ffn_in, conv_w, conv_b, seq):
    t = h2.shape[0]
    tiles_per_seq = seq // FI_TM
    halo_blocks = FI_TM // HALO
    const = lambda i: (0, 0)
    return pl.pallas_call(
        functools.partial(_ffnin_kernel, tiles_per_seq=tiles_per_seq),
        grid=(t // FI_TM,),
        in_specs=[
            pl.BlockSpec((FI_TM, D_MODEL), lambda i: (i, 0)),
            pl.BlockSpec((HALO, D_MODEL), lambda i: (jnp.maximum(i * halo_blocks - 1, 0), 0)),
            pl.BlockSpec((D_MODEL, 2 * D_FF), const),
            pl.BlockSpec((FFN_CONV_WIDTH, 2 * D_FF), const),
            pl.BlockSpec((1, 2 * D_FF), const),
        ],
        out_specs=pl.BlockSpec((FI_TM, D_FF), lambda i: (i, 0)),
        out_shape=jax.ShapeDtypeStruct((t, D_FF), BF16),
        scratch_shapes=[pltpu.VMEM((4, FI_TM + HALO, FI_TN), F32)],
        compiler_params=pltpu.CompilerParams(
            dimension_semantics=("parallel",), vmem_limit_bytes=VMEM_LIMIT),
        name="ffnin",
    )(h2, h2, w_ffn_in, conv_w, conv_b)


FO_T = 512


def _ffnout_kernel(a_ref, w_ref, x1_ref, mod_ref, nw_ref, o_ref):
    y = _dot(a_ref[...], w_ref[...])
    m = mod_ref[0]
    x2 = x1_ref[...] + m[5:6, :] * y
    ms = jnp.mean(x2 * x2, axis=-1, keepdims=True)
    o_ref[...] = x2 * lax.rsqrt(ms + NORM_EPS) * nw_ref[...]


def _ffnout(act, w_ffn_out, x1, mod3, final_w, seq):
    t = x1.shape[0]
    tiles_per_seq = seq // FO_T
    return pl.pallas_call(
        _ffnout_kernel,
        grid=(t // FO_T,),
        in_specs=[
            pl.BlockSpec((FO_T, D_FF), lambda i: (i, 0)),
            pl.BlockSpec((D_FF, D_MODEL), lambda i: (0, 0)),
            pl.BlockSpec((FO_T, D_MODEL), lambda i: (i, 0)),
            pl.BlockSpec((1, 6, D_MODEL), lambda i: (i // tiles_per_seq, 0, 0)),
            pl.BlockSpec((1, D_MODEL), lambda i: (0, 0)),
        ],
        out_specs=pl.BlockSpec((FO_T, D_MODEL), lambda i: (i, 0)),
        out_shape=jax.ShapeDtypeStruct((t, D_MODEL), F32),
        compiler_params=pltpu.CompilerParams(
            dimension_semantics=("parallel",), vmem_limit_bytes=VMEM_LIMIT),
        name="ffnout",
    )(act, w_ffn_out, x1, mod3, final_w)


def kernel(x, c, w_ada, b_ada, norm1_w, w_in, dn_conv_w, dn_A_log, dn_dt_bias, dn_norm_w,
           w_proj_sb, w_proj_dn, w_out, norm2_w, w_ffn_in, ffn_conv_w, ffn_conv_b, w_ffn_out,
           final_norm_w):
    batch, seq, d = x.shape
    depth = w_ada.shape[0]
    assert depth == 1, "the final rmsnorm is fused into the last layer's FFN-out kernel"
    t = batch * seq
    xt = x.reshape(t, d)
    c_pad = jnp.pad(c, ((0, 8 - batch), (0, 0)))

    sb_end = 3 * SB_WIDTH
    dn_end = sb_end + DN_CONV_CH
    z_end = dn_end + DN_V_WIDTH
    tail_end = z_end + 2 * DN_HEADS

    for l in range(depth):
        wl = w_in[l]
        w_main = jnp.concatenate(
            [wl[:, sb_end:dn_end], wl[:, dn_end:z_end], wl[:, tail_end:], wl[:, :sb_end]],
            axis=1).astype(BF16)
        w_tail = jnp.pad(wl[:, z_end:tail_end], ((0, 0), (0, LANES - 2 * DN_HEADS))).astype(BF16)
        alog_pad = jnp.pad(dn_A_log[l], (DN_HEADS, LANES - 2 * DN_HEADS)).reshape(1, LANES)
        dtb_pad = jnp.pad(dn_dt_bias[l], (DN_HEADS, LANES - 2 * DN_HEADS)).reshape(1, LANES)

        mod = _ada(c_pad, w_ada[l], b_ada[l].reshape(1, -1))
        mod3 = mod[:batch].reshape(batch, 6, d)

        proj, tail = _inproj(xt, mod3, norm1_w[l].reshape(1, d), w_main, w_tail, seq)
        o_a = _sb_attention(proj, batch, seq)
        qn, kn, vv, bg = _dnprep(proj, tail, dn_conv_w[l], alog_pad, dtb_pad, seq)
        u, wqk, intra, eg = _dnlocal(qn, kn, vv, bg, batch, seq)
        eg_flat = eg[:, 0, ::DN_KEY_DIM].reshape(-1)
        o_b = _dnscan(eg_flat, u, wqk, intra, proj, dn_norm_w[l].reshape(1, -1), batch, seq)
        x1, h2 = _merge(o_a, o_b, proj, xt, mod3, w_proj_sb[l].astype(BF16),
                        w_proj_dn[l].astype(BF16), w_out[l].astype(BF16),
                        norm2_w[l].reshape(1, d), seq)
        act = _ffnin(h2, w_ffn_in[l].astype(BF16), ffn_conv_w[l], ffn_conv_b[l].reshape(1, -1), seq)
        xt = _ffnout(act, w_ffn_out[l].astype(BF16), x1, mod3, final_norm_w.reshape(1, d), seq)
    return xt.reshape(batch, seq, d)
```

```python
import functools

import jax
import jax.numpy as jnp
from jax import lax
from jax.experimental import pallas as pl
from jax.experimental.pallas import tpu as pltpu

F32 = jnp.float32
BF16 = jnp.bfloat16

D_MODEL = 1024
SB_HEADS = 8
SB_HEAD_DIM = 64
SB_WIDTH = SB_HEADS * SB_HEAD_DIM
DN_HEADS = 8
DN_KEY_DIM = 64
DN_VAL_DIM = 128
DN_QK_WIDTH = DN_HEADS * DN_KEY_DIM
DN_V_WIDTH = DN_HEADS * DN_VAL_DIM
DN_CONV_CH = 2 * DN_QK_WIDTH + DN_V_WIDTH
DN_CONV_WIDTH = 4
DN_CHUNK = 64
D_FF = 2816
FFN_CONV_WIDTH = 3
NORM_EPS = 1e-6
L2_EPS = 1e-6

LANES = 128
HALO = 16

COL_DN = 0
COL_Z = DN_CONV_CH
COL_GATE = COL_Z + DN_V_WIDTH
COL_SB = COL_GATE + 2 * D_MODEL
PROJ_WIDTH = COL_SB + 3 * SB_WIDTH

VMEM_LIMIT = 56 * 1024 * 1024


def _dot(a, b):
    return jnp.dot(a, b, preferred_element_type=F32)


def _dot_nt(a, b):
    return lax.dot_general(a, b, (((1,), (1,)), ((), ())), preferred_element_type=F32)


def _dot_tn(a, b):
    return lax.dot_general(a, b, (((0,), (0,)), ((), ())), preferred_element_type=F32)


def _split2(a):
    hi = a.astype(BF16)
    lo = (a - hi.astype(F32)).astype(BF16)
    return hi, lo


def _split3(a):
    a1 = a.astype(BF16)
    r = a - a1.astype(F32)
    a2 = r.astype(BF16)
    a3 = (r - a2.astype(F32)).astype(BF16)
    return a1, a2, a3


def _dot_exact_lhs(m, b):
    b1, b2, b3 = _split3(b)
    return _dot(m, b1) + (_dot(m, b2) + _dot(m, b3))


def _dot_exact_rhs2(a, m):
    hi, lo = _split2(a)
    return _dot(hi, m) + _dot(lo, m)


def _dot3(a, b):
    a1, a2 = _split2(a)
    b1, b2 = _split2(b)
    return _dot(a1, b1) + (_dot(a1, b2) + _dot(a2, b1))


def _sigmoid(x):
    return 1.0 / (1.0 + jnp.exp(-x))


def _softplus(x):
    return jnp.maximum(x, 0.0) + jnp.log(1.0 + jnp.exp(-jnp.abs(x)))


def _ada_kernel(c_ref, w_ref, b_ref, o_ref):
    c = c_ref[...]
    ca = c * _sigmoid(c)
    o_ref[...] = _dot3(ca, w_ref[...]) + b_ref[...]


def _ada(c_pad, w_ada, b_ada):
    n = w_ada.shape[1]
    tn = 1536
    return pl.pallas_call(
        _ada_kernel,
        grid=(n // tn,),
        in_specs=[
            pl.BlockSpec((8, D_MODEL), lambda j: (0, 0)),
            pl.BlockSpec((D_MODEL, tn), lambda j: (0, j)),
            pl.BlockSpec((1, tn), lambda j: (0, j)),
        ],
        out_specs=pl.BlockSpec((8, tn), lambda j: (0, j)),
        out_shape=jax.ShapeDtypeStruct((8, n), F32),
        compiler_params=pltpu.CompilerParams(
            dimension_semantics=("arbitrary",), vmem_limit_bytes=VMEM_LIMIT),
        name="ada",
    )(c_pad, w_ada, b_ada)


def _inproj_kernel(x_ref, mod_ref, nw_ref, w_ref, wt_ref, o_ref, ot_ref, h_ref):
    @pl.when(pl.program_id(1) == 0)
    def _():
        x = x_ref[...]
        ms = jnp.mean(x * x, axis=-1, keepdims=True)
        y = x * lax.rsqrt(ms + NORM_EPS) * nw_ref[...]
        m = mod_ref[0]
        h = (y * (1.0 + m[1:2, :]) + m[0:1, :]).astype(BF16)
        h_ref[...] = h
        ot_ref[...] = _dot(h, wt_ref[...])

    o_ref[...] = _dot(h_ref[...], w_ref[...]).astype(o_ref.dtype)


def _inproj(x2, mod3, norm_w, w_main, w_tail, seq):
    t = x2.shape[0]
    tm, tn = 1024, PROJ_WIDTH // 2
    tiles_per_seq = seq // tm
    return pl.pallas_call(
        _inproj_kernel,
        grid=(t // tm, PROJ_WIDTH // tn),
        in_specs=[
            pl.BlockSpec((tm, D_MODEL), lambda i, j: (i, 0)),
            pl.BlockSpec((1, 6, D_MODEL), lambda i, j: (i // tiles_per_seq, 0, 0)),
            pl.BlockSpec((1, D_MODEL), lambda i, j: (0, 0)),
            pl.BlockSpec((D_MODEL, tn), lambda i, j: (0, j)),
            pl.BlockSpec((D_MODEL, LANES), lambda i, j: (0, 0)),
        ],
        out_specs=[
            pl.BlockSpec((tm, tn), lambda i, j: (i, j)),
            pl.BlockSpec((tm, LANES), lambda i, j: (i, 0)),
        ],
        out_shape=[
            jax.ShapeDtypeStruct((t, PROJ_WIDTH), BF16),
            jax.ShapeDtypeStruct((t, LANES), F32),
        ],
        scratch_shapes=[pltpu.VMEM((tm, D_MODEL), BF16)],
        compiler_params=pltpu.CompilerParams(
            dimension_semantics=("parallel", "arbitrary"), vmem_limit_bytes=VMEM_LIMIT),
        name="inproj",
    )(x2, mod3, norm_w, w_main, w_tail)


SB_T = 256
SB_DEAD_LOG = -100.0
SB_NO_TILE = -1e30


def _sb_kernel(q_ref, k_ref, v_ref, o_ref):
    i = pl.program_id(2)
    q = q_ref[...]
    lane = lax.broadcasted_iota(jnp.int32, (1, LANES), 1)
    first = lane < SB_HEAD_DIM
    qs = q * jnp.asarray(SB_HEAD_DIM ** -0.5, BF16)
    zero = jnp.zeros_like(qs)
    H = SB_T // 2
    qh = (jnp.where(first, qs, zero), jnp.where(first, zero, qs))
    q_chunks = [qh[e][s * H:(s + 1) * H] for e in range(2) for s in range(2)]
    r = lax.broadcasted_iota(jnp.int32, (SB_T, SB_T), 0)
    c = lax.broadcasted_iota(jnp.int32, (SB_T, SB_T), 1)
    upper = jnp.where(r > c, 1.0, 0.0).astype(BF16)
    rh = lax.broadcasted_iota(jnp.int32, (H, SB_T), 0)
    ch = lax.broadcasted_iota(jnp.int32, (H, SB_T), 1)
    mask_lo = (ch < rh)[:, :H]
    mask_hi = ch < rh + H

    def log_terms(qcs, kbs, ups, masks):
        z = [_dot_nt(qc, kb) for qc, kb in zip(qcs, kbs)]
        sp = [_softplus(x) for x in z]
        l1 = [-x if m is None else jnp.where(m, -x, 0.0) for x, m in zip(sp, masks)]
        cs = [_dot(x.astype(BF16), up) for x, up in zip(l1, ups)]
        logw = [(x - s) + y for x, s, y in zip(z, sp, cs)]
        rowsum = [jnp.sum(x, axis=-1, keepdims=True) for x in l1]
        return logw, rowsum

    def weighted_values(logw, carries, masks, vbs):
        a = [jnp.exp(x + cr) for x, cr in zip(logw, carries)]
        a = [x if m is None else jnp.where(m, x, 0.0) for x, m in zip(a, masks)]
        return [_dot(x.astype(BF16), vb) for x, vb in zip(a, vbs)]

    off_d = pl.multiple_of(i * SB_T, SB_T)
    k_d = k_ref[pl.ds(off_d, SB_T), :]
    v_d = v_ref[pl.ds(off_d, SB_T), :]
    halves = (0, 1, 0, 1)
    masks_d = [mask_hi if s else mask_lo for s in halves]
    logw, carry = log_terms(q_chunks, [k_d if s else k_d[:H] for s in halves],
                            [upper if s else upper[:H, :H] for s in halves], masks_d)
    o = weighted_values(logw, [jnp.zeros((H, 1), F32)] * 4, masks_d,
                        [v_d if s else v_d[:H] for s in halves])

    def live(carries):
        m = jnp.maximum(jnp.maximum(carries[0], carries[1]), jnp.maximum(carries[2], carries[3]))
        return jnp.max(m) > SB_DEAD_LOG

    def cond(st):
        n, go = st[0], st[1]
        return jnp.logical_and(2 * n < i, go)

    def body(st):
        n, _, o, carry = st
        j_near = i - 1 - 2 * n
        has_far = j_near > 0
        off_n = pl.multiple_of(j_near * SB_T, SB_T)
        off_f = pl.multiple_of(jnp.maximum(j_near - 1, 0) * SB_T, SB_T)
        kbs = [k_ref[pl.ds(off_n, SB_T), :]] * 4 + [k_ref[pl.ds(off_f, SB_T), :]] * 4
        vbs = [v_ref[pl.ds(off_n, SB_T), :]] * 4 + [v_ref[pl.ds(off_f, SB_T), :]] * 4
        logw, rs = log_terms(q_chunks * 2, kbs, [upper] * 8, [None] * 8)
        carry_far = [jnp.where(has_far, c + x, SB_NO_TILE) for c, x in zip(carry, rs[:4])]
        ov = weighted_values(logw, carry + carry_far, [None] * 8, vbs)
        o = [x + y + w for x, y, w in zip(o, ov[:4], ov[4:])]
        carry = [c + x + jnp.where(has_far, y, 0.0) for c, x, y in zip(carry, rs[:4], rs[4:])]
        return n + 1, live(carry), o, carry

    st = lax.while_loop(cond, body, (jnp.int32(0), live(carry), o, carry))
    o = st[2]
    o_ref[0:H, :] = jnp.where(first, o[0], o[2]).astype(o_ref.dtype)
    o_ref[H:, :] = jnp.where(first, o[1], o[3]).astype(o_ref.dtype)


def _sb_attention(proj, batch, seq):
    t = proj.shape[0]
    nq = seq // SB_T
    pairs = SB_HEADS // 2
    qcol = COL_SB // LANES
    kcol = qcol + SB_WIDTH // LANES
    vcol = kcol + SB_WIDTH // LANES
    return pl.pallas_call(
        _sb_kernel,
        grid=(batch, pairs, nq),
        in_specs=[
            pl.BlockSpec((SB_T, LANES), lambda b, p, i: (b * nq + i, qcol + p)),
            pl.BlockSpec((seq, LANES), lambda b, p, i: (b, kcol + p)),
            pl.BlockSpec((seq, LANES), lambda b, p, i: (b, vcol + p)),
        ],
        out_specs=pl.BlockSpec((SB_T, LANES), lambda b, p, i: (b * nq + i, p)),
        out_shape=jax.ShapeDtypeStruct((t, SB_WIDTH), BF16),
        compiler_params=pltpu.CompilerParams(
            dimension_semantics=("parallel", "parallel", "arbitrary"),
            vmem_limit_bytes=VMEM_LIMIT),
        name="sb",
    )(proj, proj, proj)


DNP_T = 512


def _dnprep_kernel(cur_ref, halo_ref, tail_ref, cw_ref, alog_ref, dtb_ref,
                   q_ref, k_ref, v_ref, bg_ref, xs_ref, *, tiles_per_seq):
    i = pl.program_id(0)
    cur = cur_ref[...].astype(F32)
    prev = halo_ref[...].astype(F32)
    prev = jnp.where(i % tiles_per_seq == 0, 0.0, prev)
    xs_ref[0:HALO, :] = prev
    xs_ref[HALO:, :] = cur
    w = cw_ref[...]
    y = w[DN_CONV_WIDTH - 1:DN_CONV_WIDTH, :] * cur
    for d in range(1, DN_CONV_WIDTH):
        y = y + w[DN_CONV_WIDTH - 1 - d:DN_CONV_WIDTH - d, :] * xs_ref[pl.ds(HALO - d, DNP_T), :]
    s = y * _sigmoid(y)

    r = lax.broadcasted_iota(jnp.int32, (DN_QK_WIDTH, DN_QK_WIDTH), 0) // DN_KEY_DIM
    c = lax.broadcasted_iota(jnp.int32, (DN_QK_WIDTH, DN_QK_WIDTH), 1) // DN_KEY_DIM
    same_head = jnp.where(r == c, 1.0, 0.0).astype(BF16)

    def l2n(t):
        ss = _dot_exact_rhs2(t * t, same_head)
        return t * lax.rsqrt(ss + L2_EPS)

    q = s[:, 0:DN_QK_WIDTH]
    k = s[:, DN_QK_WIDTH:2 * DN_QK_WIDTH]
    q_ref[...] = (l2n(q) * (DN_KEY_DIM ** -0.5)).astype(q_ref.dtype)
    k_ref[...] = l2n(k).astype(k_ref.dtype)
    v_ref[...] = s[:, 2 * DN_QK_WIDTH:].astype(v_ref.dtype)

    tl = tail_ref[...]
    lane = lax.broadcasted_iota(jnp.int32, (1, LANES), 1)
    beta = _sigmoid(tl)
    g = -jnp.exp(alog_ref[...]) * _softplus(tl + dtb_ref[...])
    bg_ref[...] = jnp.where(lane < DN_HEADS, beta, g)


def _dnprep(proj, tail, conv_w, alog_pad, dtb_pad, seq):
    t = proj.shape[0]
    tiles_per_seq = seq // DNP_T
    halo_blocks = DNP_T // HALO
    return pl.pallas_call(
        functools.partial(_dnprep_kernel, tiles_per_seq=tiles_per_seq),
        grid=(t // DNP_T,),
        in_specs=[
            pl.BlockSpec((DNP_T, DN_CONV_CH), lambda i: (i, 0)),
            pl.BlockSpec((HALO, DN_CONV_CH), lambda i: (jnp.maximum(i * halo_blocks - 1, 0), 0)),
            pl.BlockSpec((DNP_T, LANES), lambda i: (i, 0)),
            pl.BlockSpec((DN_CONV_WIDTH, DN_CONV_CH), lambda i: (0, 0)),
            pl.BlockSpec((1, LANES), lambda i: (0, 0)),
            pl.BlockSpec((1, LANES), lambda i: (0, 0)),
        ],
        out_specs=[
            pl.BlockSpec((DNP_T, DN_QK_WIDTH), lambda i: (i, 0)),
            pl.BlockSpec((DNP_T, DN_QK_WIDTH), lambda i: (i, 0)),
            pl.BlockSpec((DNP_T, DN_V_WIDTH), lambda i: (i, 0)),
            pl.BlockSpec((DNP_T, LANES), lambda i: (i, 0)),
        ],
        out_shape=[
            jax.ShapeDtypeStruct((t, DN_QK_WIDTH), BF16),
            jax.ShapeDtypeStruct((t, DN_QK_WIDTH), BF16),
            jax.ShapeDtypeStruct((t, DN_V_WIDTH), BF16),
            jax.ShapeDtypeStruct((t, LANES), F32),
        ],
        scratch_shapes=[pltpu.VMEM((DNP_T + HALO, DN_CONV_CH), F32)],
        compiler_params=pltpu.CompilerParams(
            dimension_semantics=("parallel",), vmem_limit_bytes=VMEM_LIMIT),
        name="dnprep",
    )(proj, proj, tail, conv_w, alog_pad, dtb_pad)


DN_GROUP = 4
DN_GROUP_LANES = DN_GROUP * DN_KEY_DIM
DN_ROWS = DN_HEADS * DN_CHUNK
DNL_CHUNKS = 4


def _bd4(x, mask):
    zero = jnp.zeros((), x.dtype)
    return jnp.concatenate([jnp.where(m, x, zero) for m in mask], axis=0)


def _dot3_bd(a, b, mask):
    a1, a2 = _split2(a)
    b1, b2 = _split2(b)
    bd1 = _bd4(b1, mask)
    bd2 = _bd4(b2, mask)
    return _dot(a1, bd1) + (_dot(a1, bd2) + _dot(a2, bd1))


def _dot1_bd(a, b, mask):
    return _dot(a.astype(BF16), _bd4(b.astype(BF16), mask))


def _dnlocal_kernel(q_ref, k_ref, v_ref, bg_ref, u_ref, wqk_ref, intra_ref, eg_ref):
    C = DN_CHUNK
    GL = DN_GROUP_LANES
    ri = lax.broadcasted_iota(jnp.int32, (C, GL), 0)
    cj = lax.broadcasted_iota(jnp.int32, (C, GL), 1) % C
    incl = ri >= cj
    strict = ri > cj
    after = jnp.where(strict, 1.0, 0.0)
    eye = jnp.where(ri == cj, 1.0, 0.0)
    blk = {b: (ri // b) == (cj // b) for b in (8, 16, 32, 64)}
    r2 = lax.broadcasted_iota(jnp.int32, (C, C), 0)
    c2 = lax.broadcasted_iota(jnp.int32, (C, C), 1)
    lower_incl = jnp.where(r2 >= c2, 1.0, 0.0).astype(BF16)
    head_of_lane = lax.broadcasted_iota(jnp.int32, (1, GL), 1) // C
    bdmask = [head_of_lane == r for r in range(DN_GROUP)]
    lane = lax.broadcasted_iota(jnp.int32, (1, LANES), 1)
    first = lane < DN_KEY_DIM
    sels = (first, jnp.logical_not(first))

    chains = [(c, g) for c in range(DNL_CHUNKS) for g in range(DN_HEADS // DN_GROUP)]
    rows_of = lambda c: slice(c * C, (c + 1) * C)
    lanes_of = lambda g: slice(g * GL, (g + 1) * GL)
    each = lambda f, *ls: [f(*xs) for xs in zip(*ls)]

    def head_bcast(c, g, col0):
        bgc = bg_ref[rows_of(c), :]
        return [jnp.broadcast_to(bgc[:, col0 + h:col0 + h + 1], (C, LANES))
                for h in range(g * DN_GROUP, (g + 1) * DN_GROUP)]

    def cat(hs):
        return jnp.concatenate([jnp.where(first, hs[0], hs[1]), jnp.where(first, hs[2], hs[3])], axis=1)

    beta_h = [head_bcast(c, g, 0) for c, g in chains]
    beta_cat = each(cat, beta_h)
    g_cat = [cat(head_bcast(c, g, DN_HEADS)) for c, g in chains]
    k_cat = [k_ref[rows_of(c), lanes_of(g)] for c, g in chains]
    q_cat = [q_ref[rows_of(c), lanes_of(g)] for c, g in chains]

    gc = [_dot_exact_lhs(lower_incl, x) for x in g_cat]
    diff = [_dot_exact_lhs(lower_incl, x * after) for x in g_cat]
    kq = each(lambda k, q: _dot_nt(jnp.concatenate([k, q], axis=0), _bd4(k, bdmask)), k_cat, q_cat)
    decay = [jnp.where(incl, jnp.exp(x), 0.0) for x in diff]
    lmat = each(lambda x, b, d: jnp.where(strict, x[:C] * b * d, 0.0), kq, beta_cat, decay)
    for (c, g), x, d in zip(chains, kq, decay):
        intra_ref[rows_of(c), lanes_of(g)] = jnp.where(incl, x[C:] * d, 0.0).astype(intra_ref.dtype)

    pw = [jnp.where(blk[8], -x, 0.0) for x in lmat]
    tinv = [eye + x for x in pw]
    p2 = [_dot3_bd(x, x, bdmask) for x in pw]
    both = each(lambda t, p: _dot3_bd(jnp.concatenate([t, p], axis=0), p, bdmask), tinv, p2)
    tinv = each(lambda t, x: t + x[:C], tinv, both)
    tinv = each(lambda t, x: t + _dot3_bd(t, x[C:], bdmask), tinv, both)
    for b in (8, 16, 32):
        offd = jnp.logical_and(blk[2 * b], jnp.logical_not(blk[b]))
        y = each(lambda l, t: _dot1_bd(jnp.where(offd, l, 0.0), t, bdmask), lmat, tinv)
        tinv = each(lambda t, x: t - _dot1_bd(t, x, bdmask), tinv, y)
    t_bd = [_bd4(t.astype(BF16), bdmask) for t in tinv]

    rhs = []
    qk_out = []
    for (c, g), gcx, k, q, bcat, bh in zip(chains, gc, k_cat, q_cat, beta_cat, beta_h):
        egc = jnp.exp(gcx)
        g_last = gcx[C - 1:C, :]
        kf = k.astype(F32)
        kbe = kf * bcat * egc
        qg = q.astype(F32) * egc
        kdec = kf * jnp.exp(g_last - gcx)
        eg_ref[c, :, lanes_of(g)] = jnp.exp(g_last)
        rhs_rows = []
        qk_rows = []
        for hh in range(DN_GROUP):
            h = g * DN_GROUP + hh
            slab = slice((hh // 2) * LANES, (hh // 2 + 1) * LANES)
            sel = sels[hh % 2]
            vb = v_ref[rows_of(c), h * DN_VAL_DIM:(h + 1) * DN_VAL_DIM].astype(F32) * bh[hh]
            rhs_rows.append(jnp.concatenate([vb, jnp.where(sel, kbe[:, slab], 0.0)], axis=1))
            qk_rows.append(jnp.concatenate([jnp.where(sel, qg[:, slab], 0.0),
                                            jnp.where(sel, kdec[:, slab], 0.0)], axis=1))
        rhs.append(jnp.concatenate(rhs_rows, axis=0).astype(BF16))
        qk_out.append(jnp.concatenate(qk_rows, axis=0))
    uw = each(_dot, t_bd, rhs)
    for (c, g), x, qkx in zip(chains, uw, qk_out):
        rows_g = slice(g * DN_GROUP * C, (g + 1) * DN_GROUP * C)
        u_ref[c, rows_g, :] = x[:, :DN_VAL_DIM]
        wqk_ref[c, rows_g, 0:LANES] = x[:, DN_VAL_DIM:].astype(wqk_ref.dtype)
        wqk_ref[c, rows_g, LANES:] = qkx.astype(wqk_ref.dtype)


def _dnlocal(qn, kn, v, bg, batch, seq):
    ntot = batch * seq // DN_CHUNK
    nsteps = ntot // DNL_CHUNKS
    rows = DNL_CHUNKS * DN_CHUNK
    row = lambda i: (i, 0)
    blk3 = lambda i: (i, 0, 0)
    return pl.pallas_call(
        _dnlocal_kernel,
        grid=(nsteps,),
        in_specs=[
            pl.BlockSpec((rows, DN_QK_WIDTH), row),
            pl.BlockSpec((rows, DN_QK_WIDTH), row),
            pl.BlockSpec((rows, DN_V_WIDTH), row),
            pl.BlockSpec((rows, LANES), row),
        ],
        out_specs=[
            pl.BlockSpec((DNL_CHUNKS, DN_ROWS, DN_VAL_DIM), blk3),
            pl.BlockSpec((DNL_CHUNKS, DN_ROWS, 3 * LANES), blk3),
            pl.BlockSpec((rows, DN_QK_WIDTH), row),
            pl.BlockSpec((DNL_CHUNKS, 1, DN_QK_WIDTH), blk3),
        ],
        out_shape=[
            jax.ShapeDtypeStruct((ntot, DN_ROWS, DN_VAL_DIM), F32),
            jax.ShapeDtypeStruct((ntot, DN_ROWS, 3 * LANES), BF16),
            jax.ShapeDtypeStruct((ntot * DN_CHUNK, DN_QK_WIDTH), BF16),
            jax.ShapeDtypeStruct((ntot, 1, DN_QK_WIDTH), F32),
        ],
        compiler_params=pltpu.CompilerParams(
            dimension_semantics=("parallel",), vmem_limit_bytes=VMEM_LIMIT),
        name="dnlocal",
    )(qn, kn, v, bg)


def _dnscan_kernel(eg_ref, u_ref, wqk_ref, intra_ref, z_ref, nw_ref, o_ref, state_ref, *,
                   batch, nchunks):
    C = DN_CHUNK
    n = pl.program_id(0)

    @pl.when(n == 0)
    def _():
        state_ref[...] = jnp.zeros_like(state_ref)

    rb = lax.broadcasted_iota(jnp.int32, (2 * C, LANES), 0) // C
    cb = lax.broadcasted_iota(jnp.int32, (2 * C, LANES), 1) // C
    bd2 = rb == cb
    nw = nw_ref[...]

    chains = [(b, p) for b in range(batch) for p in range(DN_HEADS // 2)]
    rows = lambda p: slice(2 * p * C, (2 * p + 2) * C)
    sidx = lambda b, p, e: b * DN_HEADS + 2 * p + e
    s_old = [(state_ref[sidx(b, p, 0)], state_ref[sidx(b, p, 1)]) for b, p in chains]
    s_bf = [jnp.concatenate(s, axis=0).astype(BF16) for s in s_old]
    wq = [jnp.concatenate([wqk_ref[b, 0, rows(p), 0:LANES], wqk_ref[b, 0, rows(p), LANES:2 * LANES]], axis=0)
          for b, p in chains]
    ws_qs = [_dot(a, s) for a, s in zip(wq, s_bf)]
    v_new = [(u_ref[b, 0, rows(p), :] - x[:2 * C]).astype(BF16) for (b, p), x in zip(chains, ws_qs)]
    ibd = []
    for b, p in chains:
        islab = intra_ref[b, :, p * LANES:(p + 1) * LANES]
        ibd.append(jnp.where(bd2, jnp.concatenate([islab, islab], axis=0), jnp.zeros((), islab.dtype)))
    o = [x[2 * C:] + _dot(a, v) for x, a, v in zip(ws_qs, ibd, v_new)]
    kv = [_dot_tn(wqk_ref[b, 0, rows(p), 2 * LANES:], v) for (b, p), v in zip(chains, v_new)]
    for (b, p), s, x in zip(chains, s_old, kv):
        base = (b * nchunks + n) * DN_HEADS + 2 * p
        state_ref[sidx(b, p, 0)] = s[0] * eg_ref[base] + x[:C]
        state_ref[sidx(b, p, 1)] = s[1] * eg_ref[base + 1] + x[C:]
    for (b, p), x in zip(chains, o):
        for e in range(2):
            h = 2 * p + e
            oh = x[e * C:(e + 1) * C]
            ms = jnp.mean(oh * oh, axis=-1, keepdims=True)
            zh = z_ref[b, :, h * DN_VAL_DIM:(h + 1) * DN_VAL_DIM].astype(F32)
            on = oh * lax.rsqrt(ms + NORM_EPS) * nw * (zh * _sigmoid(zh))
            o_ref[b, :, h * DN_VAL_DIM:(h + 1) * DN_VAL_DIM] = on.astype(o_ref.dtype)


def _dnscan(eg_flat, u, wqk, intra, proj, dn_norm_w, batch, seq):
    nchunks = seq // DN_CHUNK
    zcol = COL_Z // DN_V_WIDTH
    blk4 = lambda n: (0, n, 0, 0)
    out = pl.pallas_call(
        functools.partial(_dnscan_kernel, batch=batch, nchunks=nchunks),
        grid=(nchunks,),
        in_specs=[
            pl.BlockSpec(memory_space=pltpu.SMEM),
            pl.BlockSpec((batch, 1, DN_ROWS, DN_VAL_DIM), blk4),
            pl.BlockSpec((batch, 1, DN_ROWS, 3 * LANES), blk4),
            pl.BlockSpec((batch, DN_CHUNK, DN_QK_WIDTH), lambda n: (0, n, 0)),
            pl.BlockSpec((batch, DN_CHUNK, DN_V_WIDTH), lambda n: (0, n, zcol)),
            pl.BlockSpec((1, DN_VAL_DIM), lambda n: (0, 0)),
        ],
        out_specs=pl.BlockSpec((batch, DN_CHUNK, DN_V_WIDTH), lambda n: (0, n, 0)),
        out_shape=jax.ShapeDtypeStruct((batch, seq, DN_V_WIDTH), BF16),
        scratch_shapes=[pltpu.VMEM((batch * DN_HEADS, DN_KEY_DIM, DN_VAL_DIM), F32)],
        compiler_params=pltpu.CompilerParams(
            dimension_semantics=("arbitrary",), vmem_limit_bytes=VMEM_LIMIT),
        name="dnscan",
    )(eg_flat,
      u.reshape(batch, nchunks, DN_ROWS, DN_VAL_DIM),
      wqk.reshape(batch, nchunks, DN_ROWS, 3 * LANES),
      intra.reshape(batch, seq, DN_QK_WIDTH),
      proj.reshape(batch, seq, PROJ_WIDTH),
      dn_norm_w)
    return out.reshape(batch * seq, DN_V_WIDTH)


MG_T = 512


def _merge_kernel(oa_ref, ob_ref, ga_ref, gb_ref, x_ref, mod_ref, wsb_ref, wdn_ref, wout_ref,
                  nw_ref, x1_ref, h2_ref):
    pa = _dot(oa_ref[...], wsb_ref[...])
    pb = _dot(ob_ref[...], wdn_ref[...])
    merged = _sigmoid(ga_ref[...].astype(F32)) * pa + _sigmoid(gb_ref[...].astype(F32)) * pb
    y = _dot(merged.astype(BF16), wout_ref[...])
    m = mod_ref[0]
    x1 = x_ref[...] + m[2:3, :] * y
    x1_ref[...] = x1
    ms = jnp.mean(x1 * x1, axis=-1, keepdims=True)
    hn = x1 * lax.rsqrt(ms + NORM_EPS) * nw_ref[...]
    h2_ref[...] = (hn * (1.0 + m[4:5, :]) + m[3:4, :]).astype(h2_ref.dtype)


def _merge(o_a, o_b, proj, x2, mod3, w_sb, w_dn, w_out, norm2_w, seq):
    t = x2.shape[0]
    tiles_per_seq = seq // MG_T
    gcol = COL_GATE // D_MODEL
    const = lambda i: (0, 0)
    return pl.pallas_call(
        _merge_kernel,
        grid=(t // MG_T,),
        in_specs=[
            pl.BlockSpec((MG_T, SB_WIDTH), lambda i: (i, 0)),
            pl.BlockSpec((MG_T, DN_V_WIDTH), lambda i: (i, 0)),
            pl.BlockSpec((MG_T, D_MODEL), lambda i: (i, gcol)),
            pl.BlockSpec((MG_T, D_MODEL), lambda i: (i, gcol + 1)),
            pl.BlockSpec((MG_T, D_MODEL), lambda i: (i, 0)),
            pl.BlockSpec((1, 6, D_MODEL), lambda i: (i // tiles_per_seq, 0, 0)),
            pl.BlockSpec((SB_WIDTH, D_MODEL), const),
            pl.BlockSpec((DN_V_WIDTH, D_MODEL), const),
            pl.BlockSpec((D_MODEL, D_MODEL), const),
            pl.BlockSpec((1, D_MODEL), const),
        ],
        out_specs=[
            pl.BlockSpec((MG_T, D_MODEL), lambda i: (i, 0)),
            pl.BlockSpec((MG_T, D_MODEL), lambda i: (i, 0)),
        ],
        out_shape=[
            jax.ShapeDtypeStruct((t, D_MODEL), F32),
            jax.ShapeDtypeStruct((t, D_MODEL), BF16),
        ],
        compiler_params=pltpu.CompilerParams(
            dimension_semantics=("parallel",), vmem_limit_bytes=VMEM_LIMIT),
        name="merge",
    )(o_a, o_b, proj, proj, x2, mod3, w_sb, w_dn, w_out, norm2_w)


FI_TM = 512
FI_TN = 256


def _ffnin_kernel(h_ref, halo_ref, w_ref, cw_ref, b_ref, o_ref, s_ref, *, tiles_per_seq):
    i = pl.program_id(0)
    h = h_ref[...]
    hp = halo_ref[...]
    keep = jnp.where(i % tiles_per_seq == 0, 0.0, 1.0)
    nsub = D_FF // FI_TN

    def cols(br, j):
        return slice(br * D_FF + j * FI_TN, br * D_FF + (j + 1) * FI_TN)

    def matmuls(j):
        for br in range(2):
            w = w_ref[:, cols(br, j)]
            s_ref[2 * (j % 2) + br, 0:HALO, :] = _dot(hp, w) * keep
            s_ref[2 * (j % 2) + br, HALO:, :] = _dot(h, w)

    def conv_act(j):
        ys = []
        for br in range(2):
            slot = 2 * (j % 2) + br
            cw = cw_ref[:, cols(br, j)]
            y = b_ref[:, cols(br, j)]
            for d in range(FFN_CONV_WIDTH):
                y = y + cw[FFN_CONV_WIDTH - 1 - d:FFN_CONV_WIDTH - d, :] * s_ref[slot, pl.ds(HALO - d, FI_TM), :]
            ys.append(y)
        o_ref[:, j * FI_TN:(j + 1) * FI_TN] = (ys[0] * _sigmoid(ys[0]) * ys[1]).astype(o_ref.dtype)

    matmuls(0)
    for j in range(nsub):
        if j + 1 < nsub:
            matmuls(j + 1)
        conv_act(j)


def _ffnin(h2, w_ffn_in, conv_w, conv_b, seq):
    t = h2.shape[0]
    tiles_per_seq = seq // FI_TM
    halo_blocks = FI_TM // HALO
    const = lambda i: (0, 0)
    return pl.pallas_call(
        functools.partial(_ffnin_kernel, tiles_per_seq=tiles_per_seq),
        grid=(t // FI_TM,),
        in_specs=[
            pl.BlockSpec((FI_TM, D_MODEL), lambda i: (i, 0)),
            pl.BlockSpec((HALO, D_MODEL), lambda i: (jnp.maximum(i * halo_blocks - 1, 0), 0)),
            pl.BlockSpec((D_MODEL, 2 * D_FF), const),
            pl.BlockSpec((FFN_CONV_WIDTH, 2 * D_FF), const),
            pl.BlockSpec((1, 2 * D_FF), const),
        ],
        out_specs=pl.BlockSpec((FI_TM, D_FF), lambda i: (i, 0)),
        out_shape=jax.ShapeDtypeStruct((t, D_FF), BF16),
        scratch_shapes=[pltpu.VMEM((4, FI_TM + HALO, FI_TN), F32)],
        compiler_params=pltpu.CompilerParams(
            dimension_semantics=("parallel",), vmem_limit_bytes=VMEM_LIMIT),
        name="ffnin",
    )(h2, h2, w_ffn_in, conv_w, conv_b)


FO_T = 512


def _ffnout_kernel(a_ref, w_ref, x1_ref, mod_ref, nw_ref, o_ref):
    y = _dot(a_ref[...], w_ref[...])
    m = mod_ref[0]
    x2 = x1_ref[...] + m[5:6, :] * y
    ms = jnp.mean(x2 * x2, axis=-1, keepdims=True)
    o_ref[...] = x2 * lax.rsqrt(ms + NORM_EPS) * nw_ref[...]


def _ffnout(act, w_ffn_out, x1, mod3, final_w, seq):
    t = x1.shape[0]
    tiles_per_seq = seq // FO_T
    return pl.pallas_call(
        _ffnout_kernel,
        grid=(t // FO_T,),
        in_specs=[
            pl.BlockSpec((FO_T, D_FF), lambda i: (i, 0)),
            pl.BlockSpec((D_FF, D_MODEL), lambda i: (0, 0)),
            pl.BlockSpec((FO_T, D_MODEL), lambda i: (i, 0)),
            pl.BlockSpec((1, 6, D_MODEL), lambda i: (i // tiles_per_seq, 0, 0)),
            pl.BlockSpec((1, D_MODEL), lambda i: (0, 0)),
        ],
        out_specs=pl.BlockSpec((FO_T, D_MODEL), lambda i: (i, 0)),
        out_shape=jax.ShapeDtypeStruct((t, D_MODEL), F32),
        compiler_params=pltpu.CompilerParams(
            dimension_semantics=("parallel",), vmem_limit_bytes=VMEM_LIMIT),
        name="ffnout",
    )(act, w_ffn_out, x1, mod3, final_w)


def kernel(x, c, w_ada, b_ada, norm1_w, w_in, dn_conv_w, dn_A_log, dn_dt_bias, dn_norm_w,
           w_proj_sb, w_proj_dn, w_out, norm2_w, w_ffn_in, ffn_conv_w, ffn_conv_b, w_ffn_out,
           final_norm_w):
    batch, seq, d = x.shape
    depth = w_ada.shape[0]
    assert depth == 1, "the final rmsnorm is fused into the last layer's FFN-out kernel"
    t = batch * seq
    xt = x.reshape(t, d)
    c_pad = jnp.pad(c, ((0, 8 - batch), (0, 0)))

    sb_end = 3 * SB_WIDTH
    dn_end = sb_end + DN_CONV_CH
    z_end = dn_end + DN_V_WIDTH
    tail_end = z_end + 2 * DN_HEADS

    for l in range(depth):
        wl = w_in[l]
        w_main = jnp.concatenate(
            [wl[:, sb_end:dn_end], wl[:, dn_end:z_end], wl[:, tail_end:], wl[:, :sb_end]],
            axis=1).astype(BF16)
        w_tail = jnp.pad(wl[:, z_end:tail_end], ((0, 0), (0, LANES - 2 * DN_HEADS))).astype(BF16)
        alog_pad = jnp.pad(dn_A_log[l], (DN_HEADS, LANES - 2 * DN_HEADS)).reshape(1, LANES)
        dtb_pad = jnp.pad(dn_dt_bias[l], (DN_HEADS, LANES - 2 * DN_HEADS)).reshape(1, LANES)

        mod = _ada(c_pad, w_ada[l], b_ada[l].reshape(1, -1))
        mod3 = mod[:batch].reshape(batch, 6, d)

        proj, tail = _inproj(xt, mod3, norm1_w[l].reshape(1, d), w_main, w_tail, seq)
        o_a = _sb_attention(proj, batch, seq)
        qn, kn, vv, bg = _dnprep(proj, tail, dn_conv_w[l], alog_pad, dtb_pad, seq)
        u, wqk, intra, eg = _dnlocal(qn, kn, vv, bg, batch, seq)
        eg_flat = eg[:, 0, ::DN_KEY_DIM].reshape(-1)
        o_b = _dnscan(eg_flat, u, wqk, intra, proj, dn_norm_w[l].reshape(1, -1), batch, seq)
        x1, h2 = _merge(o_a, o_b, proj, xt, mod3, w_proj_sb[l].astype(BF16),
                        w_proj_dn[l].astype(BF16), w_out[l].astype(BF16),
                        norm2_w[l].reshape(1, d), seq)
        act = _ffnin(h2, w_ffn_in[l].astype(BF16), ffn_conv_w[l], ffn_conv_b[l].reshape(1, -1), seq)
        xt = _ffnout(act, w_ffn_out[l].astype(BF16), x1, mod3, final_norm_w.reshape(1, d), seq)
    return xt.reshape(batch, seq, d)
```

```python
import functools

import jax
import jax.numpy as jnp
from jax import lax
from jax.experimental import pallas as pl
from jax.experimental.pallas import tpu as pltpu

F32 = jnp.float32
BF16 = jnp.bfloat16

D_MODEL = 1024
SB_HEADS = 8
SB_HEAD_DIM = 64
SB_WIDTH = SB_HEADS * SB_HEAD_DIM
DN_HEADS = 8
DN_KEY_DIM = 64
DN_VAL_DIM = 128
DN_QK_WIDTH = DN_HEADS * DN_KEY_DIM
DN_V_WIDTH = DN_HEADS * DN_VAL_DIM
DN_CONV_CH = 2 * DN_QK_WIDTH + DN_V_WIDTH
DN_CONV_WIDTH = 4
DN_CHUNK = 64
D_FF = 2816
FFN_CONV_WIDTH = 3
NORM_EPS = 1e-6
L2_EPS = 1e-6

LANES = 128
HALO = 16

COL_DN = 0
COL_Z = DN_CONV_CH
COL_GATE = COL_Z + DN_V_WIDTH
COL_SB = COL_GATE + 2 * D_MODEL
PROJ_WIDTH = COL_SB + 3 * SB_WIDTH

VMEM_LIMIT = 56 * 1024 * 1024


def _dot(a, b):
    return jnp.dot(a, b, preferred_element_type=F32)


def _dot_nt(a, b):
    return lax.dot_general(a, b, (((1,), (1,)), ((), ())), preferred_element_type=F32)


def _dot_tn(a, b):
    return lax.dot_general(a, b, (((0,), (0,)), ((), ())), preferred_element_type=F32)


def _split2(a):
    hi = a.astype(BF16)
    lo = (a - hi.astype(F32)).astype(BF16)
    return hi, lo


def _split3(a):
    a1 = a.astype(BF16)
    r = a - a1.astype(F32)
    a2 = r.astype(BF16)
    a3 = (r - a2.astype(F32)).astype(BF16)
    return a1, a2, a3


def _dot_exact_lhs(m, b):
    b1, b2, b3 = _split3(b)
    return _dot(m, b1) + (_dot(m, b2) + _dot(m, b3))


def _dot3(a, b):
    a1, a2 = _split2(a)
    b1, b2 = _split2(b)
    return _dot(a1, b1) + (_dot(a1, b2) + _dot(a2, b1))


def _sigmoid(x):
    return 1.0 / (1.0 + jnp.exp(-x))


def _softplus(x):
    return jnp.maximum(x, 0.0) + jnp.log(1.0 + jnp.exp(-jnp.abs(x)))


def _ada_kernel(c_ref, w_ref, b_ref, o_ref):
    c = c_ref[...]
    ca = c * _sigmoid(c)
    o_ref[...] = _dot3(ca, w_ref[...]) + b_ref[...]


def _ada(c_pad, w_ada, b_ada):
    n = w_ada.shape[1]
    tn = 1536
    return pl.pallas_call(
        _ada_kernel,
        grid=(n // tn,),
        in_specs=[
            pl.BlockSpec((8, D_MODEL), lambda j: (0, 0)),
            pl.BlockSpec((D_MODEL, tn), lambda j: (0, j)),
            pl.BlockSpec((1, tn), lambda j: (0, j)),
        ],
        out_specs=pl.BlockSpec((8, tn), lambda j: (0, j)),
        out_shape=jax.ShapeDtypeStruct((8, n), F32),
        compiler_params=pltpu.CompilerParams(
            dimension_semantics=("arbitrary",), vmem_limit_bytes=VMEM_LIMIT),
        name="ada",
    )(c_pad, w_ada, b_ada)


def _inproj_kernel(x_ref, mod_ref, nw_ref, w_ref, wt_ref, o_ref, ot_ref, h_ref):
    @pl.when(pl.program_id(1) == 0)
    def _():
        x = x_ref[...]
        ms = jnp.mean(x * x, axis=-1, keepdims=True)
        y = x * lax.rsqrt(ms + NORM_EPS) * nw_ref[...]
        m = mod_ref[0]
        h = (y * (1.0 + m[1:2, :]) + m[0:1, :]).astype(BF16)
        h_ref[...] = h
        ot_ref[...] = _dot(h, wt_ref[...])

    o_ref[...] = _dot(h_ref[...], w_ref[...]).astype(o_ref.dtype)


def _inproj(x2, mod3, norm_w, w_main, w_tail, seq):
    t = x2.shape[0]
    tm, tn = 1024, PROJ_WIDTH // 2
    tiles_per_seq = seq // tm
    return pl.pallas_call(
        _inproj_kernel,
        grid=(t // tm, PROJ_WIDTH // tn),
        in_specs=[
            pl.BlockSpec((tm, D_MODEL), lambda i, j: (i, 0)),
            pl.BlockSpec((1, 6, D_MODEL), lambda i, j: (i // tiles_per_seq, 0, 0)),
            pl.BlockSpec((1, D_MODEL), lambda i, j: (0, 0)),
            pl.BlockSpec((D_MODEL, tn), lambda i, j: (0, j)),
            pl.BlockSpec((D_MODEL, LANES), lambda i, j: (0, 0)),
        ],
        out_specs=[
            pl.BlockSpec((tm, tn), lambda i, j: (i, j)),
            pl.BlockSpec((tm, LANES), lambda i, j: (i, 0)),
        ],
        out_shape=[
            jax.ShapeDtypeStruct((t, PROJ_WIDTH), BF16),
            jax.ShapeDtypeStruct((t, LANES), F32),
        ],
        scratch_shapes=[pltpu.VMEM((tm, D_MODEL), BF16)],
        compiler_params=pltpu.CompilerParams(
            dimension_semantics=("parallel", "arbitrary"), vmem_limit_bytes=VMEM_LIMIT),
        name="inproj",
    )(x2, mod3, norm_w, w_main, w_tail)


SB_T = 256
SB_DEAD_LOG = -100.0
SB_NO_TILE = -1e30


def _sb_kernel(q_ref, k_ref, v_ref, upper_ref, o_ref):
    i = pl.program_id(2)
    q = q_ref[...]
    lane = lax.broadcasted_iota(jnp.int32, (1, LANES), 1)
    first = lane < SB_HEAD_DIM
    qs = q * jnp.asarray(SB_HEAD_DIM ** -0.5, BF16)
    zero = jnp.zeros_like(qs)
    H = SB_T // 2
    qh = (jnp.where(first, qs, zero), jnp.where(first, zero, qs))
    q_chunks = [qh[e][s * H:(s + 1) * H] for e in range(2) for s in range(2)]
    upper = upper_ref[...]
    rh = lax.broadcasted_iota(jnp.int32, (H, SB_T), 0)
    ch = lax.broadcasted_iota(jnp.int32, (H, SB_T), 1)
    mask_lo = (ch < rh)[:, :H]
    mask_hi = ch < rh + H

    def log_terms(qcs, kbs, ups, masks):
        z = [_dot_nt(qc, kb) for qc, kb in zip(qcs, kbs)]
        sp = [_softplus(x) for x in z]
        l1 = [-x if m is None else jnp.where(m, -x, 0.0) for x, m in zip(sp, masks)]
        cs = [_dot(x.astype(BF16), up) for x, up in zip(l1, ups)]
        logw = [(x - s) + y for x, s, y in zip(z, sp, cs)]
        rowsum = [jnp.sum(x, axis=-1, keepdims=True) for x in l1]
        return logw, rowsum

    def weighted_values(logw, carries, masks, vbs):
        a = [jnp.exp(x + cr) for x, cr in zip(logw, carries)]
        a = [x if m is None else jnp.where(m, x, 0.0) for x, m in zip(a, masks)]
        return [_dot(x.astype(BF16), vb) for x, vb in zip(a, vbs)]

    def tile(ref, j):
        return ref[pl.ds(pl.multiple_of(jnp.maximum(j, 0) * SB_T, SB_T), SB_T), :]

    k_d, v_d = tile(k_ref, i), tile(v_ref, i)
    halves = (0, 1, 0, 1)
    masks = [mask_hi if s else mask_lo for s in halves] + [None] * 8
    logw, rs = log_terms(q_chunks * 3,
                         [k_d if s else k_d[:H] for s in halves] + [tile(k_ref, i - 1)] * 4 + [tile(k_ref, i - 2)] * 4,
                         [upper if s else upper[:H, :H] for s in halves] + [upper] * 8, masks)
    carry_near = [jnp.where(i > 0, x, SB_NO_TILE) for x in rs[:4]]
    carry_far = [jnp.where(i > 1, x + y, SB_NO_TILE) for x, y in zip(rs[:4], rs[4:8])]
    ov = weighted_values(logw, [jnp.zeros((H, 1), F32)] * 4 + carry_near + carry_far, masks,
                         [v_d if s else v_d[:H] for s in halves] + [tile(v_ref, i - 1)] * 4 + [tile(v_ref, i - 2)] * 4)
    o = [x + y + w for x, y, w in zip(ov[:4], ov[4:8], ov[8:])]
    carry = [x + jnp.where(i > 0, y, 0.0) + jnp.where(i > 1, w, 0.0) for x, y, w in zip(rs[:4], rs[4:8], rs[8:])]

    def live(carries):
        m = jnp.maximum(jnp.maximum(carries[0], carries[1]), jnp.maximum(carries[2], carries[3]))
        return jnp.max(m) > SB_DEAD_LOG

    def cond(st):
        n, go = st[0], st[1]
        return jnp.logical_and(2 * n < i, go)

    def body(st):
        n, _, o, carry = st
        j_near = i - 1 - 2 * n
        has_far = j_near > 0
        kbs = [tile(k_ref, j_near)] * 4 + [tile(k_ref, j_near - 1)] * 4
        vbs = [tile(v_ref, j_near)] * 4 + [tile(v_ref, j_near - 1)] * 4
        logw, rs = log_terms(q_chunks * 2, kbs, [upper] * 8, [None] * 8)
        carry_far = [jnp.where(has_far, c + x, SB_NO_TILE) for c, x in zip(carry, rs[:4])]
        ov = weighted_values(logw, carry + carry_far, [None] * 8, vbs)
        o = [x + y + w for x, y, w in zip(o, ov[:4], ov[4:])]
        carry = [c + x + jnp.where(has_far, y, 0.0) for c, x, y in zip(carry, rs[:4], rs[4:])]
        return n + 1, live(carry), o, carry

    st = lax.while_loop(cond, body, (jnp.int32(1), live(carry), o, carry))
    o = st[2]
    o_ref[0:H, :] = jnp.where(first, o[0], o[2]).astype(o_ref.dtype)
    o_ref[H:, :] = jnp.where(first, o[1], o[3]).astype(o_ref.dtype)


def _sb_attention(proj, batch, seq):
    t = proj.shape[0]
    nq = seq // SB_T
    pairs = SB_HEADS // 2
    qcol = COL_SB // LANES
    kcol = qcol + SB_WIDTH // LANES
    vcol = kcol + SB_WIDTH // LANES
    return pl.pallas_call(
        _sb_kernel,
        grid=(batch, pairs, nq),
        in_specs=[
            pl.BlockSpec((SB_T, LANES), lambda b, p, i: (b * nq + i, qcol + p)),
            pl.BlockSpec((seq, LANES), lambda b, p, i: (b, kcol + p)),
            pl.BlockSpec((seq, LANES), lambda b, p, i: (b, vcol + p)),
            pl.BlockSpec((SB_T, SB_T), lambda b, p, i: (0, 0)),
        ],
        out_specs=pl.BlockSpec((SB_T, LANES), lambda b, p, i: (b * nq + i, p)),
        out_shape=jax.ShapeDtypeStruct((t, SB_WIDTH), BF16),
        compiler_params=pltpu.CompilerParams(
            dimension_semantics=("parallel", "parallel", "arbitrary"),
            vmem_limit_bytes=VMEM_LIMIT),
        name="sb",
    )(proj, proj, proj, jnp.tril(jnp.ones((SB_T, SB_T), BF16), -1))


DNP_T = 512


def _dnprep_kernel(cur_ref, halo_ref, tail_ref, cw_ref, alog_ref, dtb_ref,
                   q_ref, k_ref, v_ref, bg_ref, xs_ref, *, tiles_per_seq):
    i = pl.program_id(0)
    cur = cur_ref[...].astype(F32)
    prev = halo_ref[...].astype(F32)
    prev = jnp.where(i % tiles_per_seq == 0, 0.0, prev)
    xs_ref[0:HALO, :] = prev
    xs_ref[HALO:, :] = cur
    w = cw_ref[...]
    y = w[DN_CONV_WIDTH - 1:DN_CONV_WIDTH, :] * cur
    for d in range(1, DN_CONV_WIDTH):
        y = y + w[DN_CONV_WIDTH - 1 - d:DN_CONV_WIDTH - d, :] * xs_ref[pl.ds(HALO - d, DNP_T), :]
    s = y * _sigmoid(y)

    r = lax.broadcasted_iota(jnp.int32, (DN_QK_WIDTH, DN_QK_WIDTH), 0) // DN_KEY_DIM
    c = lax.broadcasted_iota(jnp.int32, (DN_QK_WIDTH, DN_QK_WIDTH), 1) // DN_KEY_DIM
    same_head = jnp.where(r == c, 1.0, 0.0).astype(BF16)

    def l2n(t):
        ss = _dot((t * t).astype(BF16), same_head)
        return t * lax.rsqrt(ss + L2_EPS)

    q = s[:, 0:DN_QK_WIDTH]
    k = s[:, DN_QK_WIDTH:2 * DN_QK_WIDTH]
    q_ref[...] = (l2n(q) * (DN_KEY_DIM ** -0.5)).astype(q_ref.dtype)
    k_ref[...] = l2n(k).astype(k_ref.dtype)
    v_ref[...] = s[:, 2 * DN_QK_WIDTH:].astype(v_ref.dtype)

    tl = tail_ref[...]
    lane = lax.broadcasted_iota(jnp.int32, (1, LANES), 1)
    beta = _sigmoid(tl)
    g = -jnp.exp(alog_ref[...]) * _softplus(tl + dtb_ref[...])
    bg_ref[...] = jnp.where(lane < DN_HEADS, beta, g)


def _dnprep(proj, tail, conv_w, alog_pad, dtb_pad, seq):
    t = proj.shape[0]
    tiles_per_seq = seq // DNP_T
    halo_blocks = DNP_T // HALO
    return pl.pallas_call(
        functools.partial(_dnprep_kernel, tiles_per_seq=tiles_per_seq),
        grid=(t // DNP_T,),
        in_specs=[
            pl.BlockSpec((DNP_T, DN_CONV_CH), lambda i: (i, 0)),
            pl.BlockSpec((HALO, DN_CONV_CH), lambda i: (jnp.maximum(i * halo_blocks - 1, 0), 0)),
            pl.BlockSpec((DNP_T, LANES), lambda i: (i, 0)),
            pl.BlockSpec((DN_CONV_WIDTH, DN_CONV_CH), lambda i: (0, 0)),
            pl.BlockSpec((1, LANES), lambda i: (0, 0)),
            pl.BlockSpec((1, LANES), lambda i: (0, 0)),
        ],
        out_specs=[
            pl.BlockSpec((DNP_T, DN_QK_WIDTH), lambda i: (i, 0)),
            pl.BlockSpec((DNP_T, DN_QK_WIDTH), lambda i: (i, 0)),
            pl.BlockSpec((DNP_T, DN_V_WIDTH), lambda i: (i, 0)),
            pl.BlockSpec((DNP_T, LANES), lambda i: (i, 0)),
        ],
        out_shape=[
            jax.ShapeDtypeStruct((t, DN_QK_WIDTH), BF16),
            jax.ShapeDtypeStruct((t, DN_QK_WIDTH), BF16),
            jax.ShapeDtypeStruct((t, DN_V_WIDTH), BF16),
            jax.ShapeDtypeStruct((t, LANES), F32),
        ],
        scratch_shapes=[pltpu.VMEM((DNP_T + HALO, DN_CONV_CH), F32)],
        compiler_params=pltpu.CompilerParams(
            dimension_semantics=("parallel",), vmem_limit_bytes=VMEM_LIMIT),
        name="dnprep",
    )(proj, proj, tail, conv_w, alog_pad, dtb_pad)


DN_GROUP = 4
DN_GROUP_LANES = DN_GROUP * DN_KEY_DIM
DN_ROWS = DN_HEADS * DN_CHUNK
DNL_CHUNKS = 4


def _bd4(x, mask):
    zero = jnp.zeros((), x.dtype)
    return jnp.concatenate([jnp.where(m, x, zero) for m in mask], axis=0)


def _dot3_bd(a, b, mask):
    a1, a2 = _split2(a)
    b1, b2 = _split2(b)
    bd1 = _bd4(b1, mask)
    bd2 = _bd4(b2, mask)
    return _dot(a1, bd1) + (_dot(a1, bd2) + _dot(a2, bd1))


def _dot1_bd(a, b, mask):
    return _dot(a.astype(BF16), _bd4(b.astype(BF16), mask))


def _dnlocal_kernel(q_ref, k_ref, v_ref, bg_ref, u_ref, wqk_ref, intra_ref, eg_ref):
    C = DN_CHUNK
    GL = DN_GROUP_LANES
    ri = lax.broadcasted_iota(jnp.int32, (C, GL), 0)
    cj = lax.broadcasted_iota(jnp.int32, (C, GL), 1) % C
    incl = ri >= cj
    strict = ri > cj
    after = jnp.where(strict, 1.0, 0.0)
    eye = jnp.where(ri == cj, 1.0, 0.0)
    blk = {b: (ri // b) == (cj // b) for b in (8, 16, 32, 64)}
    r2 = lax.broadcasted_iota(jnp.int32, (C, C), 0)
    c2 = lax.broadcasted_iota(jnp.int32, (C, C), 1)
    lower_incl = jnp.where(r2 >= c2, 1.0, 0.0).astype(BF16)
    head_of_lane = lax.broadcasted_iota(jnp.int32, (1, GL), 1) // C
    bdmask = [head_of_lane == r for r in range(DN_GROUP)]
    lane = lax.broadcasted_iota(jnp.int32, (1, LANES), 1)
    first = lane < DN_KEY_DIM
    sels = (first, jnp.logical_not(first))

    chains = [(c, g) for c in range(DNL_CHUNKS) for g in range(DN_HEADS // DN_GROUP)]
    rows_of = lambda c: slice(c * C, (c + 1) * C)
    lanes_of = lambda g: slice(g * GL, (g + 1) * GL)
    each = lambda f, *ls: [f(*xs) for xs in zip(*ls)]

    def head_bcast(c, g, col0):
        bgc = bg_ref[rows_of(c), :]
        return [jnp.broadcast_to(bgc[:, col0 + h:col0 + h + 1], (C, LANES))
                for h in range(g * DN_GROUP, (g + 1) * DN_GROUP)]

    def cat(hs):
        return jnp.concatenate([jnp.where(first, hs[0], hs[1]), jnp.where(first, hs[2], hs[3])], axis=1)

    beta_h = [head_bcast(c, g, 0) for c, g in chains]
    beta_cat = each(cat, beta_h)
    g_cat = [cat(head_bcast(c, g, DN_HEADS)) for c, g in chains]
    k_cat = [k_ref[rows_of(c), lanes_of(g)] for c, g in chains]
    q_cat = [q_ref[rows_of(c), lanes_of(g)] for c, g in chains]

    gc = [_dot_exact_lhs(lower_incl, x) for x in g_cat]
    diff = [_dot_exact_lhs(lower_incl, x * after) for x in g_cat]
    kq = each(lambda k, q: _dot_nt(jnp.concatenate([k, q], axis=0), _bd4(k, bdmask)), k_cat, q_cat)
    decay = [jnp.where(incl, jnp.exp(x), 0.0) for x in diff]
    lmat = each(lambda x, b, d: jnp.where(strict, x[:C] * b * d, 0.0), kq, beta_cat, decay)
    for (c, g), x, d in zip(chains, kq, decay):
        intra_ref[rows_of(c), lanes_of(g)] = jnp.where(incl, x[C:] * d, 0.0).astype(intra_ref.dtype)

    pw = [jnp.where(blk[8], -x, 0.0) for x in lmat]
    tinv = [eye + x for x in pw]
    p2 = [_dot3_bd(x, x, bdmask) for x in pw]
    both = each(lambda t, p: _dot3_bd(jnp.concatenate([t, p], axis=0), p, bdmask), tinv, p2)
    tinv = each(lambda t, x: t + x[:C], tinv, both)
    tinv = each(lambda t, x: t + _dot3_bd(t, x[C:], bdmask), tinv, both)
    for b in (8, 16, 32):
        offd = jnp.logical_and(blk[2 * b], jnp.logical_not(blk[b]))
        y = each(lambda l, t: _dot1_bd(jnp.where(offd, l, 0.0), t, bdmask), lmat, tinv)
        tinv = each(lambda t, x: t - _dot1_bd(t, x, bdmask), tinv, y)
    t_bd = [_bd4(t.astype(BF16), bdmask) for t in tinv]

    rhs = []
    qk_out = []
    for (c, g), gcx, k, q, bcat, bh in zip(chains, gc, k_cat, q_cat, beta_cat, beta_h):
        egc = jnp.exp(gcx)
        g_last = gcx[C - 1:C, :]
        kf = k.astype(F32)
        kbe = kf * bcat * egc
        qg = q.astype(F32) * egc
        kdec = kf * jnp.exp(g_last - gcx)
        eg_ref[c, :, lanes_of(g)] = jnp.exp(g_last)
        rhs_rows = []
        qk_rows = []
        for hh in range(DN_GROUP):
            h = g * DN_GROUP + hh
            slab = slice((hh // 2) * LANES, (hh // 2 + 1) * LANES)
            sel = sels[hh % 2]
            vb = v_ref[rows_of(c), h * DN_VAL_DIM:(h + 1) * DN_VAL_DIM].astype(F32) * bh[hh]
            rhs_rows.append(jnp.concatenate([vb, jnp.where(sel, kbe[:, slab], 0.0)], axis=1))
            qk_rows.append(jnp.concatenate([jnp.where(sel, qg[:, slab], 0.0),
                                            jnp.where(sel, kdec[:, slab], 0.0)], axis=1))
        rhs.append(jnp.concatenate(rhs_rows, axis=0).astype(BF16))
        qk_out.append(jnp.concatenate(qk_rows, axis=0))
    uw = each(_dot, t_bd, rhs)
    for (c, g), x, qkx in zip(chains, uw, qk_out):
        rows_g = slice(g * DN_GROUP * C, (g + 1) * DN_GROUP * C)
        u_ref[c, rows_g, :] = x[:, :DN_VAL_DIM]
        wqk_ref[c, rows_g, 0:LANES] = x[:, DN_VAL_DIM:].astype(wqk_ref.dtype)
        wqk_ref[c, rows_g, LANES:] = qkx.astype(wqk_ref.dtype)


def _dnlocal(qn, kn, v, bg, batch, seq):
    ntot = batch * seq // DN_CHUNK
    nsteps = ntot // DNL_CHUNKS
    rows = DNL_CHUNKS * DN_CHUNK
    row = lambda i: (i, 0)
    blk3 = lambda i: (i, 0, 0)
    return pl.pallas_call(
        _dnlocal_kernel,
        grid=(nsteps,),
        in_specs=[
            pl.BlockSpec((rows, DN_QK_WIDTH), row),
            pl.BlockSpec((rows, DN_QK_WIDTH), row),
            pl.BlockSpec((rows, DN_V_WIDTH), row),
            pl.BlockSpec((rows, LANES), row),
        ],
        out_specs=[
            pl.BlockSpec((DNL_CHUNKS, DN_ROWS, DN_VAL_DIM), blk3),
            pl.BlockSpec((DNL_CHUNKS, DN_ROWS, 3 * LANES), blk3),
            pl.BlockSpec((rows, DN_QK_WIDTH), row),
            pl.BlockSpec((DNL_CHUNKS, 1, DN_QK_WIDTH), blk3),
        ],
        out_shape=[
            jax.ShapeDtypeStruct((ntot, DN_ROWS, DN_VAL_DIM), F32),
            jax.ShapeDtypeStruct((ntot, DN_ROWS, 3 * LANES), BF16),
            jax.ShapeDtypeStruct((ntot * DN_CHUNK, DN_QK_WIDTH), BF16),
            jax.ShapeDtypeStruct((ntot, 1, DN_QK_WIDTH), F32),
        ],
        compiler_params=pltpu.CompilerParams(
            dimension_semantics=("parallel",), vmem_limit_bytes=VMEM_LIMIT),
        name="dnlocal",
    )(qn, kn, v, bg)


def _dnscan_kernel(eg_ref, u_ref, wqk_ref, intra_ref, z_ref, nw_ref, o_ref, state_ref, *,
                   batch, nchunks):
    C = DN_CHUNK
    n = pl.program_id(0)

    @pl.when(n == 0)
    def _():
        state_ref[...] = jnp.zeros_like(state_ref)

    rb = lax.broadcasted_iota(jnp.int32, (2 * C, LANES), 0) // C
    cb = lax.broadcasted_iota(jnp.int32, (2 * C, LANES), 1) // C
    bd2 = rb == cb
    nw = nw_ref[...]

    chains = [(b, p) for b in range(batch) for p in range(DN_HEADS // 2)]
    rows = lambda p: slice(2 * p * C, (2 * p + 2) * C)
    sidx = lambda b, p, e: b * DN_HEADS + 2 * p + e
    s_old = [(state_ref[sidx(b, p, 0)], state_ref[sidx(b, p, 1)]) for b, p in chains]
    s_bf = [jnp.concatenate(s, axis=0).astype(BF16) for s in s_old]
    wq = [jnp.concatenate([wqk_ref[b, 0, rows(p), 0:LANES], wqk_ref[b, 0, rows(p), LANES:2 * LANES]], axis=0)
          for b, p in chains]
    ws_qs = [_dot(a, s) for a, s in zip(wq, s_bf)]
    v_new = [(u_ref[b, 0, rows(p), :] - x[:2 * C]).astype(BF16) for (b, p), x in zip(chains, ws_qs)]
    ibd = []
    for b, p in chains:
        islab = intra_ref[b, :, p * LANES:(p + 1) * LANES]
        ibd.append(jnp.where(bd2, jnp.concatenate([islab, islab], axis=0), jnp.zeros((), islab.dtype)))
    o = [x[2 * C:] + _dot(a, v) for x, a, v in zip(ws_qs, ibd, v_new)]
    kv = [_dot_tn(wqk_ref[b, 0, rows(p), 2 * LANES:], v) for (b, p), v in zip(chains, v_new)]
    for (b, p), s, x in zip(chains, s_old, kv):
        base = (b * nchunks + n) * DN_HEADS + 2 * p
        state_ref[sidx(b, p, 0)] = s[0] * eg_ref[base] + x[:C]
        state_ref[sidx(b, p, 1)] = s[1] * eg_ref[base + 1] + x[C:]
    for (b, p), x in zip(chains, o):
        for e in range(2):
            h = 2 * p + e
            oh = x[e * C:(e + 1) * C]
            ms = jnp.mean(oh * oh, axis=-1, keepdims=True)
            zh = z_ref[b, :, h * DN_VAL_DIM:(h + 1) * DN_VAL_DIM].astype(F32)
            on = oh * lax.rsqrt(ms + NORM_EPS) * nw * (zh * _sigmoid(zh))
            o_ref[b, :, h * DN_VAL_DIM:(h + 1) * DN_VAL_DIM] = on.astype(o_ref.dtype)


def _dnscan(eg_flat, u, wqk, intra, proj, dn_norm_w, batch, seq):
    nchunks = seq // DN_CHUNK
    zcol = COL_Z // DN_V_WIDTH
    blk4 = lambda n: (0, n, 0, 0)
    out = pl.pallas_call(
        functools.partial(_dnscan_kernel, batch=batch, nchunks=nchunks),
        grid=(nchunks,),
        in_specs=[
            pl.BlockSpec(memory_space=pltpu.SMEM),
            pl.BlockSpec((batch, 1, DN_ROWS, DN_VAL_DIM), blk4),
            pl.BlockSpec((batch, 1, DN_ROWS, 3 * LANES), blk4),
            pl.BlockSpec((batch, DN_CHUNK, DN_QK_WIDTH), lambda n: (0, n, 0)),
            pl.BlockSpec((batch, DN_CHUNK, DN_V_WIDTH), lambda n: (0, n, zcol)),
            pl.BlockSpec((1, DN_VAL_DIM), lambda n: (0, 0)),
        ],
        out_specs=pl.BlockSpec((batch, DN_CHUNK, DN_V_WIDTH), lambda n: (0, n, 0)),
        out_shape=jax.ShapeDtypeStruct((batch, seq, DN_V_WIDTH), BF16),
        scratch_shapes=[pltpu.VMEM((batch * DN_HEADS, DN_KEY_DIM, DN_VAL_DIM), F32)],
        compiler_params=pltpu.CompilerParams(
            dimension_semantics=("arbitrary",), vmem_limit_bytes=VMEM_LIMIT),
        name="dnscan",
    )(eg_flat,
      u.reshape(batch, nchunks, DN_ROWS, DN_VAL_DIM),
      wqk.reshape(batch, nchunks, DN_ROWS, 3 * LANES),
      intra.reshape(batch, seq, DN_QK_WIDTH),
      proj.reshape(batch, seq, PROJ_WIDTH),
      dn_norm_w)
    return out.reshape(batch * seq, DN_V_WIDTH)


MG_T = 512


def _merge_kernel(oa_ref, ob_ref, ga_ref, gb_ref, x_ref, mod_ref, wsb_ref, wdn_ref, wout_ref,
                  nw_ref, x1_ref, h2_ref):
    pa = _dot(oa_ref[...], wsb_ref[...])
    pb = _dot(ob_ref[...], wdn_ref[...])
    merged = _sigmoid(ga_ref[...].astype(F32)) * pa + _sigmoid(gb_ref[...].astype(F32)) * pb
    y = _dot(merged.astype(BF16), wout_ref[...])
    m = mod_ref[0]
    x1 = x_ref[...] + m[2:3, :] * y
    x1_ref[...] = x1
    ms = jnp.mean(x1 * x1, axis=-1, keepdims=True)
    hn = x1 * lax.rsqrt(ms + NORM_EPS) * nw_ref[...]
    h2_ref[...] = (hn * (1.0 + m[4:5, :]) + m[3:4, :]).astype(h2_ref.dtype)


def _merge(o_a, o_b, proj, x2, mod3, w_sb, w_dn, w_out, norm2_w, seq):
    t = x2.shape[0]
    tiles_per_seq = seq // MG_T
    gcol = COL_GATE // D_MODEL
    const = lambda i: (0, 0)
    return pl.pallas_call(
        _merge_kernel,
        grid=(t // MG_T,),
        in_specs=[
            pl.BlockSpec((MG_T, SB_WIDTH), lambda i: (i, 0)),
            pl.BlockSpec((MG_T, DN_V_WIDTH), lambda i: (i, 0)),
            pl.BlockSpec((MG_T, D_MODEL), lambda i: (i, gcol)),
            pl.BlockSpec((MG_T, D_MODEL), lambda i: (i, gcol + 1)),
            pl.BlockSpec((MG_T, D_MODEL), lambda i: (i, 0)),
            pl.BlockSpec((1, 6, D_MODEL), lambda i: (i // tiles_per_seq, 0, 0)),
            pl.BlockSpec((SB_WIDTH, D_MODEL), const),
            pl.BlockSpec((DN_V_WIDTH, D_MODEL), const),
            pl.BlockSpec((D_MODEL, D_MODEL), const),
            pl.BlockSpec((1, D_MODEL), const),
        ],
        out_specs=[
            pl.BlockSpec((MG_T, D_MODEL), lambda i: (i, 0)),
            pl.BlockSpec((MG_T, D_MODEL), lambda i: (i, 0)),
        ],
        out_shape=[
            jax.ShapeDtypeStruct((t, D_MODEL), F32),
            jax.ShapeDtypeStruct((t, D_MODEL), BF16),
        ],
        compiler_params=pltpu.CompilerParams(
            dimension_semantics=("parallel",), vmem_limit_bytes=VMEM_LIMIT),
        name="merge",
    )(o_a, o_b, proj, proj, x2, mod3, w_sb, w_dn, w_out, norm2_w)


FI_TM = 512
FI_TN = 256


def _ffnin_kernel(h_ref, halo_ref, w_ref, cw_ref, b_ref, o_ref, s_ref, *, tiles_per_seq):
    i = pl.program_id(0)
    h = h_ref[...]
    hp = halo_ref[...]
    keep = jnp.where(i % tiles_per_seq == 0, 0.0, 1.0)
    nsub = D_FF // FI_TN

    def cols(br, j):
        return slice(br * D_FF + j * FI_TN, br * D_FF + (j + 1) * FI_TN)

    def matmuls(j):
        for br in range(2):
            w = w_ref[:, cols(br, j)]
            s_ref[2 * (j % 2) + br, 0:HALO, :] = _dot(hp, w) * keep
            s_ref[2 * (j % 2) + br, HALO:, :] = _dot(h, w)

    def conv_act(j):
        ys = []
        for br in range(2):
            slot = 2 * (j % 2) + br
            cw = cw_ref[:, cols(br, j)]
            y = b_ref[:, cols(br, j)]
            for d in range(FFN_CONV_WIDTH):
                y = y + cw[FFN_CONV_WIDTH - 1 - d:FFN_CONV_WIDTH - d, :] * s_ref[slot, pl.ds(HALO - d, FI_TM), :]
            ys.append(y)
        o_ref[:, j * FI_TN:(j + 1) * FI_TN] = (ys[0] * _sigmoid(ys[0]) * ys[1]).astype(o_ref.dtype)

    matmuls(0)
    for j in range(nsub):
        if j + 1 < nsub:
            matmuls(j + 1)
        conv_act(j)


def _ffnin(h2, w_ffn_in, conv_w, conv_b, seq):
    t = h2.shape[0]
    tiles_per_seq = seq // FI_TM
    halo_blocks = FI_TM // HALO
    const = lambda i: (0, 0)
    return pl.pallas_call(
        functools.partial(_ffnin_kernel, tiles_per_seq=tiles_per_seq),
        grid=(t // FI_TM,),
        in_specs=[
            pl.BlockSpec((FI_TM, D_MODEL), lambda i: (i, 0)),
            pl.BlockSpec((HALO, D_MODEL), lambda i: (jnp.maximum(i * halo_blocks - 1, 0), 0)),
            pl.BlockSpec((D_MODEL, 2 * D_FF), const),
            pl.BlockSpec((FFN_CONV_WIDTH, 2 * D_FF), const),
            pl.BlockSpec((1, 2 * D_FF), const),
        ],
        out_specs=pl.BlockSpec((FI_TM, D_FF), lambda i: (i, 0)),
        out_shape=jax.ShapeDtypeStruct((t, D_FF), BF16),
        scratch_shapes=[pltpu.VMEM((4, FI_TM + HALO, FI_TN), F32)],
        compiler_params=pltpu.CompilerParams(
            dimension_semantics=("parallel",), vmem_limit_bytes=VMEM_LIMIT),
        name="ffnin",
    )(h2, h2, w_ffn_in, conv_w, conv_b)


FO_T = 512


def _ffnout_kernel(a_ref, w_ref, x1_ref, mod_ref, nw_ref, o_ref):
    y = _dot(a_ref[...], w_ref[...])
    m = mod_ref[0]
    x2 = x1_ref[...] + m[5:6, :] * y
    ms = jnp.mean(x2 * x2, axis=-1, keepdims=True)
    o_ref[...] = x2 * lax.rsqrt(ms + NORM_EPS) * nw_ref[...]


def _ffnout(act, w_ffn_out, x1, mod3, final_w, seq):
    t = x1.shape[0]
    tiles_per_seq = seq // FO_T
    return pl.pallas_call(
        _ffnout_kernel,
        grid=(t // FO_T,),
        in_specs=[
            pl.BlockSpec((FO_T, D_FF), lambda i: (i, 0)),
            pl.BlockSpec((D_FF, D_MODEL), lambda i: (0, 0)),
            pl.BlockSpec((FO_T, D_MODEL), lambda i: (i, 0)),
            pl.BlockSpec((1, 6, D_MODEL), lambda i: (i // tiles_per_seq, 0, 0)),
            pl.BlockSpec((1, D_MODEL), lambda i: (0, 0)),
        ],
        out_specs=pl.BlockSpec((FO_T, D_MODEL), lambda i: (i, 0)),
        out_shape=jax.ShapeDtypeStruct((t, D_MODEL), F32),
        compiler_params=pltpu.CompilerParams(
            dimension_semantics=("parallel",), vmem_limit_bytes=VMEM_LIMIT),
        name="ffnout",
    )(act, w_ffn_out, x1, mod3, final_w)


def kernel(x, c, w_ada, b_ada, norm1_w, w_in, dn_conv_w, dn_A_log, dn_dt_bias, dn_norm_w,
           w_proj_sb, w_proj_dn, w_out, norm2_w, w_ffn_in, ffn_conv_w, ffn_conv_b, w_ffn_out,
           final_norm_w):
    batch, seq, d = x.shape
    depth = w_ada.shape[0]
    assert depth == 1, "the final rmsnorm is fused into the last layer's FFN-out kernel"
    t = batch * seq
    xt = x.reshape(t, d)
    c_pad = jnp.pad(c, ((0, 8 - batch), (0, 0)))

    sb_end = 3 * SB_WIDTH
    dn_end = sb_end + DN_CONV_CH
    z_end = dn_end + DN_V_WIDTH
    tail_end = z_end + 2 * DN_HEADS

    for l in range(depth):
        wl = w_in[l]
        w_main = jnp.concatenate(
            [wl[:, sb_end:z_end].astype(BF16), wl[:, tail_end:].astype(BF16), wl[:, :sb_end].astype(BF16)],
            axis=1)
        w_tail = jnp.pad(wl[:, z_end:tail_end], ((0, 0), (0, LANES - 2 * DN_HEADS))).astype(BF16)
        alog_pad = jnp.pad(dn_A_log[l], (DN_HEADS, LANES - 2 * DN_HEADS)).reshape(1, LANES)
        dtb_pad = jnp.pad(dn_dt_bias[l], (DN_HEADS, LANES - 2 * DN_HEADS)).reshape(1, LANES)

        mod = _ada(c_pad, w_ada[l], b_ada[l].reshape(1, -1))
        mod3 = mod[:batch].reshape(batch, 6, d)

        proj, tail = _inproj(xt, mod3, norm1_w[l].reshape(1, d), w_main, w_tail, seq)
        o_a = _sb_attention(proj, batch, seq)
        qn, kn, vv, bg = _dnprep(proj, tail, dn_conv_w[l], alog_pad, dtb_pad, seq)
        u, wqk, intra, eg = _dnlocal(qn, kn, vv, bg, batch, seq)
        eg_flat = eg[:, 0, ::DN_KEY_DIM].reshape(-1)
        o_b = _dnscan(eg_flat, u, wqk, intra, proj, dn_norm_w[l].reshape(1, -1), batch, seq)
        x1, h2 = _merge(o_a, o_b, proj, xt, mod3, w_proj_sb[l].astype(BF16),
                        w_proj_dn[l].astype(BF16), w_out[l].astype(BF16),
                        norm2_w[l].reshape(1, d), seq)
        act = _ffnin(h2, w_ffn_in[l].astype(BF16), ffn_conv_w[l], ffn_conv_b[l].reshape(1, -1), seq)
        xt = _ffnout(act, w_ffn_out[l].astype(BF16), x1, mod3, final_norm_w.reshape(1, d), seq)
    return xt.reshape(batch, seq, d)
```

```python
import functools

import jax
import jax.numpy as jnp
from jax import lax
from jax.experimental import pallas as pl
from jax.experimental.pallas import tpu as pltpu

F32 = jnp.float32
BF16 = jnp.bfloat16

D_MODEL = 1024
SB_HEADS = 8
SB_HEAD_DIM = 64
SB_WIDTH = SB_HEADS * SB_HEAD_DIM
DN_HEADS = 8
DN_KEY_DIM = 64
DN_VAL_DIM = 128
DN_QK_WIDTH = DN_HEADS * DN_KEY_DIM
DN_V_WIDTH = DN_HEADS * DN_VAL_DIM
DN_CONV_CH = 2 * DN_QK_WIDTH + DN_V_WIDTH
DN_CONV_WIDTH = 4
DN_CHUNK = 64
D_FF = 2816
FFN_CONV_WIDTH = 3
NORM_EPS = 1e-6
L2_EPS = 1e-6

LANES = 128
HALO = 16

COL_DN = 0
COL_Z = DN_CONV_CH
COL_GATE = COL_Z + DN_V_WIDTH
COL_SB = COL_GATE + 2 * D_MODEL
PROJ_WIDTH = COL_SB + 3 * SB_WIDTH

VMEM_LIMIT = 56 * 1024 * 1024


def _dot(a, b):
    return jnp.dot(a, b, preferred_element_type=F32)


def _dot_nt(a, b):
    return lax.dot_general(a, b, (((1,), (1,)), ((), ())), preferred_element_type=F32)


def _dot_tn(a, b):
    return lax.dot_general(a, b, (((0,), (0,)), ((), ())), preferred_element_type=F32)


def _split2(a):
    hi = a.astype(BF16)
    lo = (a - hi.astype(F32)).astype(BF16)
    return hi, lo


def _split3(a):
    a1 = a.astype(BF16)
    r = a - a1.astype(F32)
    a2 = r.astype(BF16)
    a3 = (r - a2.astype(F32)).astype(BF16)
    return a1, a2, a3


def _dot_exact_lhs(m, b):
    b1, b2, b3 = _split3(b)
    return _dot(m, b1) + (_dot(m, b2) + _dot(m, b3))


def _dot3(a, b):
    a1, a2 = _split2(a)
    b1, b2 = _split2(b)
    return _dot(a1, b1) + (_dot(a1, b2) + _dot(a2, b1))


def _sigmoid(x):
    return 1.0 / (1.0 + jnp.exp(-x))


def _softplus(x):
    return jnp.maximum(x, 0.0) + jnp.log(1.0 + jnp.exp(-jnp.abs(x)))


def _ada_kernel(c_ref, w_ref, b_ref, o_ref):
    c = c_ref[...]
    ca = c * _sigmoid(c)
    o_ref[...] = _dot3(ca, w_ref[...]) + b_ref[...]


def _ada(c_pad, w_ada, b_ada):
    n = w_ada.shape[1]
    tn = 1536
    return pl.pallas_call(
        _ada_kernel,
        grid=(n // tn,),
        in_specs=[
            pl.BlockSpec((8, D_MODEL), lambda j: (0, 0)),
            pl.BlockSpec((D_MODEL, tn), lambda j: (0, j)),
            pl.BlockSpec((1, tn), lambda j: (0, j)),
        ],
        out_specs=pl.BlockSpec((8, tn), lambda j: (0, j)),
        out_shape=jax.ShapeDtypeStruct((8, n), F32),
        compiler_params=pltpu.CompilerParams(
            dimension_semantics=("arbitrary",), vmem_limit_bytes=VMEM_LIMIT),
        name="ada",
    )(c_pad, w_ada, b_ada)


def _inproj_kernel(x_ref, mod_ref, nw_ref, w_ref, wt_ref, o_ref, ot_ref, h_ref):
    @pl.when(pl.program_id(1) == 0)
    def _():
        x = x_ref[...]
        ms = jnp.mean(x * x, axis=-1, keepdims=True)
        y = x * lax.rsqrt(ms + NORM_EPS) * nw_ref[...]
        m = mod_ref[0]
        h = (y * (1.0 + m[1:2, :]) + m[0:1, :]).astype(BF16)
        h_ref[...] = h
        ot_ref[...] = _dot(h, wt_ref[...])

    o_ref[...] = _dot(h_ref[...], w_ref[...]).astype(o_ref.dtype)


def _inproj(x2, mod3, norm_w, w_main, w_tail, seq):
    t = x2.shape[0]
    tm, tn = 1024, PROJ_WIDTH // 2
    tiles_per_seq = seq // tm
    return pl.pallas_call(
        _inproj_kernel,
        grid=(t // tm, PROJ_WIDTH // tn),
        in_specs=[
            pl.BlockSpec((tm, D_MODEL), lambda i, j: (i, 0)),
            pl.BlockSpec((1, 6, D_MODEL), lambda i, j: (i // tiles_per_seq, 0, 0)),
            pl.BlockSpec((1, D_MODEL), lambda i, j: (0, 0)),
            pl.BlockSpec((D_MODEL, tn), lambda i, j: (0, j)),
            pl.BlockSpec((D_MODEL, LANES), lambda i, j: (0, 0)),
        ],
        out_specs=[
            pl.BlockSpec((tm, tn), lambda i, j: (i, j)),
            pl.BlockSpec((tm, LANES), lambda i, j: (i, 0)),
        ],
        out_shape=[
            jax.ShapeDtypeStruct((t, PROJ_WIDTH), BF16),
            jax.ShapeDtypeStruct((t, LANES), F32),
        ],
        scratch_shapes=[pltpu.VMEM((tm, D_MODEL), BF16)],
        compiler_params=pltpu.CompilerParams(
            dimension_semantics=("parallel", "arbitrary"), vmem_limit_bytes=VMEM_LIMIT),
        name="inproj",
    )(x2, mod3, norm_w, w_main, w_tail)


SB_T = 256
SB_DEAD_LOG = -100.0
SB_NO_TILE = -1e30


SB_TILES = 2


def _sb_kernel(q_ref, k_ref, v_ref, upper_ref, o_ref):
    upper = upper_ref[...]
    for s in range(SB_TILES):
        _sb_tile(pl.program_id(2) * SB_TILES + s, q_ref[s * SB_T:(s + 1) * SB_T, :],
                 k_ref, v_ref, upper, o_ref, s * SB_T)


def _sb_tile(i, q, k_ref, v_ref, upper, o_ref, row0):
    lane = lax.broadcasted_iota(jnp.int32, (1, LANES), 1)
    first = lane < SB_HEAD_DIM
    qs = q * jnp.asarray(SB_HEAD_DIM ** -0.5, BF16)
    zero = jnp.zeros_like(qs)
    H = SB_T // 2
    qh = (jnp.where(first, qs, zero), jnp.where(first, zero, qs))
    q_chunks = [qh[e][s * H:(s + 1) * H] for e in range(2) for s in range(2)]
    rh =lax.broadcasted_iota(jnp.int32, (H, SB_T), 0)
    ch = lax.broadcasted_iota(jnp.int32, (H, SB_T), 1)
    mask_lo = (ch < rh)[:, :H]
    mask_hi = ch < rh + H

    def log_terms(qcs, kbs, ups, masks):
        z = [_dot_nt(qc, kb) for qc, kb in zip(qcs, kbs)]
        sp = [_softplus(x) for x in z]
        l1 = [-x if m is None else jnp.where(m, -x, 0.0) for x, m in zip(sp, masks)]
        cs = [_dot(x.astype(BF16), up) for x, up in zip(l1, ups)]
        logw = [(x - s) + y for x, s, y in zip(z, sp, cs)]
        rowsum = [jnp.sum(x, axis=-1, keepdims=True) for x in l1]
        return logw, rowsum

    def weighted_values(logw, carries, masks, vbs):
        a = [jnp.exp(x + cr) for x, cr in zip(logw, carries)]
        a = [x if m is None else jnp.where(m, x, 0.0) for x, m in zip(a, masks)]
        return [_dot(x.astype(BF16), vb) for x, vb in zip(a, vbs)]

    def tile(ref, j):
        return ref[pl.ds(pl.multiple_of(jnp.maximum(j, 0) * SB_T, SB_T), SB_T), :]

    k_d, v_d = tile(k_ref, i), tile(v_ref, i)
    halves = (0, 1, 0, 1)
    masks = [mask_hi if s else mask_lo for s in halves] + [None] * 8
    logw, rs = log_terms(q_chunks * 3,
                         [k_d if s else k_d[:H] for s in halves] + [tile(k_ref, i - 1)] * 4 + [tile(k_ref, i - 2)] * 4,
                         [upper if s else upper[:H, :H] for s in halves] + [upper] * 8, masks)
    carry_near = [jnp.where(i > 0, x, SB_NO_TILE) for x in rs[:4]]
    carry_far = [jnp.where(i > 1, x + y, SB_NO_TILE) for x, y in zip(rs[:4], rs[4:8])]
    ov = weighted_values(logw, [jnp.zeros((H, 1), F32)] * 4 + carry_near + carry_far, masks,
                         [v_d if s else v_d[:H] for s in halves] + [tile(v_ref, i - 1)] * 4 + [tile(v_ref, i - 2)] * 4)
    o = [x + y + w for x, y, w in zip(ov[:4], ov[4:8], ov[8:])]
    carry = [x + jnp.where(i > 0, y, 0.0) + jnp.where(i > 1, w, 0.0) for x, y, w in zip(rs[:4], rs[4:8], rs[8:])]

    def live(carries):
        m = jnp.maximum(jnp.maximum(carries[0], carries[1]), jnp.maximum(carries[2], carries[3]))
        return jnp.max(m) > SB_DEAD_LOG

    def cond(st):
        n, go = st[0], st[1]
        return jnp.logical_and(2 * n < i, go)

    def body(st):
        n, _, o, carry = st
        j_near = i - 1 - 2 * n
        has_far = j_near > 0
        kbs = [tile(k_ref, j_near)] * 4 + [tile(k_ref, j_near - 1)] * 4
        vbs = [tile(v_ref, j_near)] * 4 + [tile(v_ref, j_near - 1)] * 4
        logw, rs = log_terms(q_chunks * 2, kbs, [upper] * 8, [None] * 8)
        carry_far = [jnp.where(has_far, c + x, SB_NO_TILE) for c, x in zip(carry, rs[:4])]
        ov = weighted_values(logw, carry + carry_far, [None] * 8, vbs)
        o = [x + y + w for x, y, w in zip(o, ov[:4], ov[4:])]
        carry = [c + x + jnp.where(has_far, y, 0.0) for c, x, y in zip(carry, rs[:4], rs[4:])]
        return n + 1, live(carry), o, carry

    st = lax.while_loop(cond, body, (jnp.int32(1), live(carry), o, carry))
    o = st[2]
    o_ref[row0:row0 + H, :] = jnp.where(first, o[0], o[2]).astype(o_ref.dtype)
    o_ref[row0 + H:row0 + SB_T, :] = jnp.where(first, o[1], o[3]).astype(o_ref.dtype)


def _sb_attention(proj, batch, seq):
    t = proj.shape[0]
    rows = SB_TILES * SB_T
    nq = seq // rows
    pairs = SB_HEADS // 2
    qcol = COL_SB // LANES
    kcol = qcol + SB_WIDTH // LANES
    vcol = kcol + SB_WIDTH // LANES
    return pl.pallas_call(
        _sb_kernel,
        grid=(batch, pairs, nq),
        in_specs=[
            pl.BlockSpec((rows, LANES), lambda b, p, i: (b * nq + i, qcol + p)),
            pl.BlockSpec((seq, LANES), lambda b, p, i: (b, kcol + p)),
            pl.BlockSpec((seq, LANES), lambda b, p, i: (b, vcol + p)),
            pl.BlockSpec((SB_T, SB_T), lambda b, p, i: (0, 0)),
        ],
        out_specs=pl.BlockSpec((rows, LANES), lambda b, p, i: (b * nq + i, p)),
        out_shape=jax.ShapeDtypeStruct((t, SB_WIDTH), BF16),
        compiler_params=pltpu.CompilerParams(
            dimension_semantics=("parallel", "parallel", "arbitrary"),
            vmem_limit_bytes=VMEM_LIMIT),
        name="sb",
    )(proj, proj, proj, jnp.tril(jnp.ones((SB_T, SB_T), BF16), -1))


DNP_T = 512


def _dnprep_kernel(cur_ref, halo_ref, tail_ref, cw_ref, alog_ref, dtb_ref,
                   q_ref, k_ref, v_ref, bg_ref, xs_ref, *, tiles_per_seq):
    i = pl.program_id(0)
    cur = cur_ref[...].astype(F32)
    prev = halo_ref[...].astype(F32)
    prev = jnp.where(i % tiles_per_seq == 0, 0.0, prev)
    xs_ref[0:HALO, :] = prev
    xs_ref[HALO:, :] = cur
    w = cw_ref[...]
    y = w[DN_CONV_WIDTH - 1:DN_CONV_WIDTH, :] * cur
    for d in range(1, DN_CONV_WIDTH):
        y = y + w[DN_CONV_WIDTH - 1 - d:DN_CONV_WIDTH - d, :] * xs_ref[pl.ds(HALO - d, DNP_T), :]
    s = y * _sigmoid(y)

    r = lax.broadcasted_iota(jnp.int32, (DN_QK_WIDTH, DN_QK_WIDTH), 0) // DN_KEY_DIM
    c = lax.broadcasted_iota(jnp.int32, (DN_QK_WIDTH, DN_QK_WIDTH), 1) // DN_KEY_DIM
    same_head = jnp.where(r == c, 1.0, 0.0).astype(BF16)

    def l2n(t):
        ss = _dot((t * t).astype(BF16), same_head)
        return t * lax.rsqrt(ss + L2_EPS)

    q = s[:, 0:DN_QK_WIDTH]
    k = s[:, DN_QK_WIDTH:2 * DN_QK_WIDTH]
    q_ref[...] = (l2n(q) * (DN_KEY_DIM ** -0.5)).astype(q_ref.dtype)
    k_ref[...] = l2n(k).astype(k_ref.dtype)
    v_ref[...] = s[:, 2 * DN_QK_WIDTH:].astype(v_ref.dtype)

    tl = tail_ref[...]
    lane = lax.broadcasted_iota(jnp.int32, (1, LANES), 1)
    beta = _sigmoid(tl)
    g = -jnp.exp(alog_ref[...]) * _softplus(tl + dtb_ref[...])
    bg_ref[...] = jnp.where(lane < DN_HEADS, beta, g)


def _dnprep(proj, tail, conv_w, alog_pad, dtb_pad, seq):
    t = proj.shape[0]
    tiles_per_seq = seq // DNP_T
    halo_blocks = DNP_T // HALO
    return pl.pallas_call(
        functools.partial(_dnprep_kernel, tiles_per_seq=tiles_per_seq),
        grid=(t // DNP_T,),
        in_specs=[
            pl.BlockSpec((DNP_T, DN_CONV_CH), lambda i: (i, 0)),
            pl.BlockSpec((HALO, DN_CONV_CH), lambda i: (jnp.maximum(i * halo_blocks - 1, 0), 0)),
            pl.BlockSpec((DNP_T, LANES), lambda i: (i, 0)),
            pl.BlockSpec((DN_CONV_WIDTH, DN_CONV_CH), lambda i: (0, 0)),
            pl.BlockSpec((1, LANES), lambda i: (0, 0)),
            pl.BlockSpec((1, LANES), lambda i: (0, 0)),
        ],
        out_specs=[
            pl.BlockSpec((DNP_T, DN_QK_WIDTH), lambda i: (i, 0)),
            pl.BlockSpec((DNP_T, DN_QK_WIDTH), lambda i: (i, 0)),
            pl.BlockSpec((DNP_T, DN_V_WIDTH), lambda i: (i, 0)),
            pl.BlockSpec((DNP_T, LANES), lambda i: (i, 0)),
        ],
        out_shape=[
            jax.ShapeDtypeStruct((t, DN_QK_WIDTH), BF16),
            jax.ShapeDtypeStruct((t, DN_QK_WIDTH), BF16),
            jax.ShapeDtypeStruct((t, DN_V_WIDTH), BF16),
            jax.ShapeDtypeStruct((t, LANES), F32),
        ],
        scratch_shapes=[pltpu.VMEM((DNP_T + HALO, DN_CONV_CH), F32)],
        compiler_params=pltpu.CompilerParams(
            dimension_semantics=("parallel",), vmem_limit_bytes=VMEM_LIMIT),
        name="dnprep",
    )(proj, proj, tail, conv_w, alog_pad, dtb_pad)


DN_GROUP = 4
DN_GROUP_LANES = DN_GROUP * DN_KEY_DIM
DN_ROWS = DN_HEADS * DN_CHUNK
DNL_CHUNKS = 8


def _bd4(x, mask):
    zero = jnp.zeros((), x.dtype)
    return jnp.concatenate([jnp.where(m, x, zero) for m in mask], axis=0)


def _dot3_bd(a, b, mask):
    a1, a2 = _split2(a)
    b1, b2 = _split2(b)
    bd1 = _bd4(b1, mask)
    bd2 = _bd4(b2, mask)
    return _dot(a1, bd1) + (_dot(a1, bd2) + _dot(a2, bd1))


def _dot1_bd(a, b, mask):
    return _dot(a.astype(BF16), _bd4(b.astype(BF16), mask))


def _dnlocal_kernel(q_ref, k_ref, v_ref, bg_ref, u_ref, wqk_ref, intra_ref, eg_ref):
    C = DN_CHUNK
    GL = DN_GROUP_LANES
    ri = lax.broadcasted_iota(jnp.int32, (C, GL), 0)
    cj = lax.broadcasted_iota(jnp.int32, (C, GL), 1) % C
    incl = ri >= cj
    strict = ri > cj
    after = jnp.where(strict, 1.0, 0.0)
    eye = jnp.where(ri == cj, 1.0, 0.0)
    blk = {b: (ri // b) == (cj // b) for b in (8, 16, 32, 64)}
    r2 = lax.broadcasted_iota(jnp.int32, (C, C), 0)
    c2 = lax.broadcasted_iota(jnp.int32, (C, C), 1)
    lower_incl = jnp.where(r2 >= c2, 1.0, 0.0).astype(BF16)
    head_of_lane = lax.broadcasted_iota(jnp.int32, (1, GL), 1) // C
    bdmask = [head_of_lane == r for r in range(DN_GROUP)]
    lane = lax.broadcasted_iota(jnp.int32, (1, LANES), 1)
    first = lane < DN_KEY_DIM
    sels = (first, jnp.logical_not(first))

    chains = [(c, g) for c in range(DNL_CHUNKS) for g in range(DN_HEADS // DN_GROUP)]
    rows_of = lambda c: slice(c * C, (c + 1) * C)
    lanes_of = lambda g: slice(g * GL, (g + 1) * GL)
    each = lambda f, *ls: [f(*xs) for xs in zip(*ls)]

    def head_bcast(c, g, col0):
        bgc = bg_ref[rows_of(c), :]
        return [jnp.broadcast_to(bgc[:, col0 + h:col0 + h + 1], (C, LANES))
                for h in range(g * DN_GROUP, (g + 1) * DN_GROUP)]

    def cat(hs):
        return jnp.concatenate([jnp.where(first, hs[0], hs[1]), jnp.where(first, hs[2], hs[3])], axis=1)

    beta_h = [head_bcast(c, g, 0) for c, g in chains]
    beta_cat = each(cat, beta_h)
    g_cat = [cat(head_bcast(c, g, DN_HEADS)) for c, g in chains]
    k_cat = [k_ref[rows_of(c), lanes_of(g)] for c, g in chains]
    q_cat = [q_ref[rows_of(c), lanes_of(g)] for c, g in chains]

    gc = [_dot_exact_lhs(lower_incl, x) for x in g_cat]
    diff = [_dot_exact_lhs(lower_incl, x * after) for x in g_cat]
    kq = each(lambda k, q: _dot_nt(jnp.concatenate([k, q], axis=0), _bd4(k, bdmask)), k_cat, q_cat)
    decay = [jnp.where(incl, jnp.exp(x), 0.0) for x in diff]
    lmat = each(lambda x, b, d: jnp.where(strict, x[:C] * b * d, 0.0), kq, beta_cat, decay)
    for (c, g), x, d in zip(chains, kq, decay):
        intra_ref[rows_of(c), lanes_of(g)] = jnp.where(incl, x[C:] * d, 0.0).astype(intra_ref.dtype)

    pw = [jnp.where(blk[8], -x, 0.0) for x in lmat]
    tinv = [eye + x for x in pw]
    p2 = [_dot3_bd(x, x, bdmask) for x in pw]
    both = each(lambda t, p: _dot3_bd(jnp.concatenate([t, p], axis=0), p, bdmask), tinv, p2)
    tinv = each(lambda t, x: t + x[:C], tinv, both)
    tinv = each(lambda t, x: t + _dot3_bd(t, x[C:], bdmask), tinv, both)
    for b in (8, 16, 32):
        offd = jnp.logical_and(blk[2 * b], jnp.logical_not(blk[b]))
        y = each(lambda l, t: _dot1_bd(jnp.where(offd, l, 0.0), t, bdmask), lmat, tinv)
        tinv = each(lambda t, x: t - _dot1_bd(t, x, bdmask), tinv, y)
    t_bd = [_bd4(t.astype(BF16), bdmask) for t in tinv]

    rhs = []
    qk_out = []
    for (c, g), gcx, k, q, bcat, bh in zip(chains, gc, k_cat, q_cat, beta_cat, beta_h):
        egc = jnp.exp(gcx)
        g_last = gcx[C - 1:C, :]
        kf = k.astype(F32)
        kbe = kf * bcat * egc
        qg = q.astype(F32) * egc
        kdec = kf * jnp.exp(g_last - gcx)
        eg_ref[c, :, lanes_of(g)] = jnp.exp(g_last)
        rhs_rows = []
        qk_rows = []
        for hh in range(DN_GROUP):
            h = g * DN_GROUP + hh
            slab = slice((hh // 2) * LANES, (hh // 2 + 1) * LANES)
            sel = sels[hh % 2]
            vb = v_ref[rows_of(c), h * DN_VAL_DIM:(h + 1) * DN_VAL_DIM].astype(F32) * bh[hh]
            rhs_rows.append(jnp.concatenate([vb, jnp.where(sel, kbe[:, slab], 0.0)], axis=1))
            qk_rows.append(jnp.concatenate([jnp.where(sel, qg[:, slab], 0.0),
                                            jnp.where(sel, kdec[:, slab], 0.0)], axis=1))
        rhs.append(jnp.concatenate(rhs_rows, axis=0).astype(BF16))
        qk_out.append(jnp.concatenate(qk_rows, axis=0))
    uw = each(_dot, t_bd, rhs)
    for (c, g), x, qkx in zip(chains, uw, qk_out):
        rows_g = slice(g * DN_GROUP * C, (g + 1) * DN_GROUP * C)
        u_ref[c, rows_g, :] = x[:, :DN_VAL_DIM]
        wqk_ref[c, rows_g, 0:LANES] = x[:, DN_VAL_DIM:].astype(wqk_ref.dtype)
        wqk_ref[c, rows_g, LANES:] = qkx.astype(wqk_ref.dtype)


def _dnlocal(qn, kn, v, bg, batch, seq):
    ntot = batch * seq // DN_CHUNK
    nsteps = ntot // DNL_CHUNKS
    rows = DNL_CHUNKS * DN_CHUNK
    row = lambda i: (i, 0)
    blk3 = lambda i: (i, 0, 0)
    return pl.pallas_call(
        _dnlocal_kernel,
        grid=(nsteps,),
        in_specs=[
            pl.BlockSpec((rows, DN_QK_WIDTH), row),
            pl.BlockSpec((rows, DN_QK_WIDTH), row),
            pl.BlockSpec((rows, DN_V_WIDTH), row),
            pl.BlockSpec((rows, LANES), row),
        ],
        out_specs=[
            pl.BlockSpec((DNL_CHUNKS, DN_ROWS, DN_VAL_DIM), blk3),
            pl.BlockSpec((DNL_CHUNKS, DN_ROWS, 3 * LANES), blk3),
            pl.BlockSpec((rows, DN_QK_WIDTH), row),
            pl.BlockSpec((DNL_CHUNKS, 1, DN_QK_WIDTH), blk3),
        ],
        out_shape=[
            jax.ShapeDtypeStruct((ntot, DN_ROWS, DN_VAL_DIM), F32),
            jax.ShapeDtypeStruct((ntot, DN_ROWS, 3 * LANES), BF16),
            jax.ShapeDtypeStruct((ntot * DN_CHUNK, DN_QK_WIDTH), BF16),
            jax.ShapeDtypeStruct((ntot, 1, DN_QK_WIDTH), F32),
        ],
        compiler_params=pltpu.CompilerParams(
            dimension_semantics=("parallel",), vmem_limit_bytes=VMEM_LIMIT),
        name="dnlocal",
    )(qn, kn, v, bg)


def _dnscan_kernel(eg_ref, u_ref, wqk_ref, intra_ref, z_ref, nw_ref, o_ref, state_ref, *,
                   batch, nchunks):
    C = DN_CHUNK
    n = pl.program_id(0)

    @pl.when(n == 0)
    def _():
        state_ref[...] = jnp.zeros_like(state_ref)

    rb = lax.broadcasted_iota(jnp.int32, (2 * C, LANES), 0) // C
    cb = lax.broadcasted_iota(jnp.int32, (2 * C, LANES), 1) // C
    bd2 = rb == cb
    nw = nw_ref[...]

    chains = [(b, p) for b in range(batch) for p in range(DN_HEADS // 2)]
    rows = lambda p: slice(2 * p * C, (2 * p + 2) * C)
    sidx = lambda b, p, e: b * DN_HEADS + 2 * p + e
    s_old = [(state_ref[sidx(b, p, 0)], state_ref[sidx(b, p, 1)]) for b, p in chains]
    s_bf = [jnp.concatenate(s, axis=0).astype(BF16) for s in s_old]
    wq = [jnp.concatenate([wqk_ref[b, 0, rows(p), 0:LANES], wqk_ref[b, 0, rows(p), LANES:2 * LANES]], axis=0)
          for b, p in chains]
    ws_qs = [_dot(a, s) for a, s in zip(wq, s_bf)]
    v_new = [(u_ref[b, 0, rows(p), :] - x[:2 * C]).astype(BF16) for (b, p), x in zip(chains, ws_qs)]
    ibd = []
    for b, p in chains:
        islab = intra_ref[b, :, p * LANES:(p + 1) * LANES]
        ibd.append(jnp.where(bd2, jnp.concatenate([islab, islab], axis=0), jnp.zeros((), islab.dtype)))
    o = [x[2 * C:] + _dot(a, v) for x, a, v in zip(ws_qs, ibd, v_new)]
    kv = [_dot_tn(wqk_ref[b, 0, rows(p), 2 * LANES:], v) for (b, p), v in zip(chains, v_new)]
    for (b, p), s, x in zip(chains, s_old, kv):
        base = (b * nchunks + n) * DN_HEADS + 2 * p
        state_ref[sidx(b, p, 0)] = s[0] * eg_ref[base] + x[:C]
        state_ref[sidx(b, p, 1)] = s[1] * eg_ref[base + 1] + x[C:]
    for (b, p), x in zip(chains, o):
        for e in range(2):
            h = 2 * p + e
            oh = x[e * C:(e + 1) * C]
            ms = jnp.mean(oh * oh, axis=-1, keepdims=True)
            zh = z_ref[b, :, h * DN_VAL_DIM:(h + 1) * DN_VAL_DIM].astype(F32)
            on = oh * lax.rsqrt(ms + NORM_EPS) * nw * (zh * _sigmoid(zh))
            o_ref[b, :, h * DN_VAL_DIM:(h + 1) * DN_VAL_DIM] = on.astype(o_ref.dtype)


def _dnscan(eg_flat, u, wqk, intra, proj, dn_norm_w, batch, seq):
    nchunks = seq // DN_CHUNK
    zcol = COL_Z // DN_V_WIDTH
    blk4 = lambda n: (0, n, 0, 0)
    out = pl.pallas_call(
        functools.partial(_dnscan_kernel, batch=batch, nchunks=nchunks),
        grid=(nchunks,),
        in_specs=[
            pl.BlockSpec(memory_space=pltpu.SMEM),
            pl.BlockSpec((batch, 1, DN_ROWS, DN_VAL_DIM), blk4),
            pl.BlockSpec((batch, 1, DN_ROWS, 3 * LANES), blk4),
            pl.BlockSpec((batch, DN_CHUNK, DN_QK_WIDTH), lambda n: (0, n, 0)),
            pl.BlockSpec((batch, DN_CHUNK, DN_V_WIDTH), lambda n: (0, n, zcol)),
            pl.BlockSpec((1, DN_VAL_DIM), lambda n: (0, 0)),
        ],
        out_specs=pl.BlockSpec((batch, DN_CHUNK, DN_V_WIDTH), lambda n: (0, n, 0)),
        out_shape=jax.ShapeDtypeStruct((batch, seq, DN_V_WIDTH), BF16),
        scratch_shapes=[pltpu.VMEM((batch * DN_HEADS, DN_KEY_DIM, DN_VAL_DIM), F32)],
        compiler_params=pltpu.CompilerParams(
            dimension_semantics=("arbitrary",), vmem_limit_bytes=VMEM_LIMIT),
        name="dnscan",
    )(eg_flat,
      u.reshape(batch, nchunks, DN_ROWS, DN_VAL_DIM),
      wqk.reshape(batch, nchunks, DN_ROWS, 3 * LANES),
      intra.reshape(batch, seq, DN_QK_WIDTH),
      proj.reshape(batch, seq, PROJ_WIDTH),
      dn_norm_w)
    return out.reshape(batch * seq, DN_V_WIDTH)


MG_T = 512


def _merge_kernel(oa_ref, ob_ref, ga_ref, gb_ref, x_ref, mod_ref, wsb_ref, wdn_ref, wout_ref,
                  nw_ref, x1_ref, h2_ref):
    pa = _dot(oa_ref[...], wsb_ref[...])
    pb = _dot(ob_ref[...], wdn_ref[...])
    merged = _sigmoid(ga_ref[...].astype(F32)) * pa + _sigmoid(gb_ref[...].astype(F32)) * pb
    y = _dot(merged.astype(BF16), wout_ref[...])
    m = mod_ref[0]
    x1 = x_ref[...] + m[2:3, :] * y
    x1_ref[...] = x1
    ms = jnp.mean(x1 * x1, axis=-1, keepdims=True)
    hn = x1 * lax.rsqrt(ms + NORM_EPS) * nw_ref[...]
    h2_ref[...] = (hn * (1.0 + m[4:5, :]) + m[3:4, :]).astype(h2_ref.dtype)


def _merge(o_a, o_b, proj, x2, mod3, w_sb, w_dn, w_out, norm2_w, seq):
    t = x2.shape[0]
    tiles_per_seq = seq // MG_T
    gcol = COL_GATE // D_MODEL
    const = lambda i: (0, 0)
    return pl.pallas_call(
        _merge_kernel,
        grid=(t // MG_T,),
        in_specs=[
            pl.BlockSpec((MG_T, SB_WIDTH), lambda i: (i, 0)),
            pl.BlockSpec((MG_T, DN_V_WIDTH), lambda i: (i, 0)),
            pl.BlockSpec((MG_T, D_MODEL), lambda i: (i, gcol)),
            pl.BlockSpec((MG_T, D_MODEL), lambda i: (i, gcol + 1)),
            pl.BlockSpec((MG_T, D_MODEL), lambda i: (i, 0)),
            pl.BlockSpec((1, 6, D_MODEL), lambda i: (i // tiles_per_seq, 0, 0)),
            pl.BlockSpec((SB_WIDTH, D_MODEL), const),
            pl.BlockSpec((DN_V_WIDTH, D_MODEL), const),
            pl.BlockSpec((D_MODEL, D_MODEL), const),
            pl.BlockSpec((1, D_MODEL), const),
        ],
        out_specs=[
            pl.BlockSpec((MG_T, D_MODEL), lambda i: (i, 0)),
            pl.BlockSpec((MG_T, D_MODEL), lambda i: (i, 0)),
        ],
        out_shape=[
            jax.ShapeDtypeStruct((t, D_MODEL), F32),
            jax.ShapeDtypeStruct((t, D_MODEL), BF16),
        ],
        compiler_params=pltpu.CompilerParams(
            dimension_semantics=("parallel",), vmem_limit_bytes=VMEM_LIMIT),
        name="merge",
    )(o_a, o_b, proj, proj, x2, mod3, w_sb, w_dn, w_out, norm2_w)


FI_TM = 512
FI_TN = 256


def _ffnin_kernel(h_ref, halo_ref, w_ref, cw_ref, b_ref, o_ref, s_ref, *, tiles_per_seq):
    i = pl.program_id(0)
    h = h_ref[...]
    hp = halo_ref[...]
    keep = jnp.where(i % tiles_per_seq == 0, 0.0, 1.0)
    nsub = D_FF // FI_TN

    def cols(br, j):
        return slice(br * D_FF + j * FI_TN, br * D_FF + (j + 1) * FI_TN)

    def matmuls(j):
        for br in range(2):
            w = w_ref[:, cols(br, j)]
            s_ref[2 * (j % 2) + br, 0:HALO, :] = _dot(hp, w) * keep
            s_ref[2 * (j % 2) + br, HALO:, :] = _dot(h, w)

    def conv_act(j):
        ys = []
        for br in range(2):
            slot = 2 * (j % 2) + br
            cw = cw_ref[:, cols(br, j)]
            y = b_ref[:, cols(br, j)]
            for d in range(FFN_CONV_WIDTH):
                y = y + cw[FFN_CONV_WIDTH - 1 - d:FFN_CONV_WIDTH - d, :] * s_ref[slot, pl.ds(HALO - d, FI_TM), :]
            ys.append(y)
        o_ref[:, j * FI_TN:(j + 1) * FI_TN] = (ys[0] * _sigmoid(ys[0]) * ys[1]).astype(o_ref.dtype)

    matmuls(0)
    for j in range(nsub):
        if j + 1 < nsub:
            matmuls(j + 1)
        conv_act(j)


def _ffnin(h2, w_ffn_in, conv_w, conv_b, seq):
    t = h2.shape[0]
    tiles_per_seq = seq // FI_TM
    halo_blocks = FI_TM // HALO
    const = lambda i: (0, 0)
    return pl.pallas_call(
        functools.partial(_ffnin_kernel, tiles_per_seq=tiles_per_seq),
        grid=(t // FI_TM,),
        in_specs=[
            pl.BlockSpec((FI_TM, D_MODEL), lambda i: (i, 0)),
            pl.BlockSpec((HALO, D_MODEL), lambda i: (jnp.maximum(i * halo_blocks - 1, 0), 0)),
            pl.BlockSpec((D_MODEL, 2 * D_FF), const),
            pl.BlockSpec((FFN_CONV_WIDTH, 2 * D_FF), const),
            pl.BlockSpec((1, 2 * D_FF), const),
        ],
        out_specs=pl.BlockSpec((FI_TM, D_FF), lambda i: (i, 0)),
        out_shape=jax.ShapeDtypeStruct((t, D_FF), BF16),
        scratch_shapes=[pltpu.VMEM((4, FI_TM + HALO, FI_TN), F32)],
        compiler_params=pltpu.CompilerParams(
            dimension_semantics=("parallel",), vmem_limit_bytes=VMEM_LIMIT),
        name="ffnin",
    )(h2, h2, w_ffn_in, conv_w, conv_b)


FO_T = 512


def _ffnout_kernel(a_ref, w_ref, x1_ref, mod_ref, nw_ref, o_ref):
    y = _dot(a_ref[...], w_ref[...])
    m = mod_ref[0]
    x2 = x1_ref[...] + m[5:6, :] * y
    ms = jnp.mean(x2 * x2, axis=-1, keepdims=True)
    o_ref[...] = x2 * lax.rsqrt(ms + NORM_EPS) * nw_ref[...]


def _ffnout(act, w_ffn_out, x1, mod3, final_w, seq):
    t = x1.shape[0]
    tiles_per_seq = seq // FO_T
    return pl.pallas_call(
        _ffnout_kernel,
        grid=(t // FO_T,),
        in_specs=[
            pl.BlockSpec((FO_T, D_FF), lambda i: (i, 0)),
            pl.BlockSpec((D_FF, D_MODEL), lambda i: (0, 0)),
            pl.BlockSpec((FO_T, D_MODEL), lambda i: (i, 0)),
            pl.BlockSpec((1, 6, D_MODEL), lambda i: (i // tiles_per_seq, 0, 0)),
            pl.BlockSpec((1, D_MODEL), lambda i: (0, 0)),
        ],
        out_specs=pl.BlockSpec((FO_T, D_MODEL), lambda i: (i, 0)),
        out_shape=jax.ShapeDtypeStruct((t, D_MODEL), F32),
        compiler_params=pltpu.CompilerParams(
            dimension_semantics=("parallel",), vmem_limit_bytes=VMEM_LIMIT),
        name="ffnout",
    )(act, w_ffn_out, x1, mod3, final_w)


def kernel(x, c, w_ada, b_ada, norm1_w, w_in, dn_conv_w, dn_A_log, dn_dt_bias, dn_norm_w,
           w_proj_sb, w_proj_dn, w_out, norm2_w, w_ffn_in, ffn_conv_w, ffn_conv_b, w_ffn_out,
           final_norm_w):
    batch, seq, d = x.shape
    depth = w_ada.shape[0]
    assert depth == 1, "the final rmsnorm is fused into the last layer's FFN-out kernel"
    t = batch * seq
    xt = x.reshape(t, d)
    c_pad = jnp.pad(c, ((0, 8 - batch), (0, 0)))

    sb_end = 3 * SB_WIDTH
    dn_end = sb_end + DN_CONV_CH
    z_end = dn_end + DN_V_WIDTH
    tail_end = z_end + 2 * DN_HEADS

    for l in range(depth):
        wl = w_in[l]
        w_main = jnp.concatenate(
            [wl[:, sb_end:z_end].astype(BF16), wl[:, tail_end:].astype(BF16), wl[:, :sb_end].astype(BF16)],
            axis=1)
        w_tail = jnp.pad(wl[:, z_end:tail_end], ((0, 0), (0, LANES - 2 * DN_HEADS))).astype(BF16)
        alog_pad = jnp.pad(dn_A_log[l], (DN_HEADS, LANES - 2 * DN_HEADS)).reshape(1, LANES)
        dtb_pad = jnp.pad(dn_dt_bias[l], (DN_HEADS, LANES - 2 * DN_HEADS)).reshape(1, LANES)

        mod = _ada(c_pad, w_ada[l], b_ada[l].reshape(1, -1))
        mod3 = mod[:batch].reshape(batch, 6, d)

        proj, tail = _inproj(xt, mod3, norm1_w[l].reshape(1, d), w_main, w_tail, seq)
        o_a = _sb_attention(proj, batch, seq)
        qn, kn, vv, bg = _dnprep(proj, tail, dn_conv_w[l], alog_pad, dtb_pad, seq)
        u, wqk, intra, eg = _dnlocal(qn, kn, vv, bg, batch, seq)
        eg_flat = eg[:, 0, ::DN_KEY_DIM].reshape(-1)
        o_b = _dnscan(eg_flat, u, wqk, intra, proj, dn_norm_w[l].reshape(1, -1), batch, seq)
        x1, h2 = _merge(o_a, o_b, proj, xt, mod3, w_proj_sb[l].astype(BF16),
                        w_proj_dn[l].astype(BF16), w_out[l].astype(BF16),
                        norm2_w[l].reshape(1, d), seq)
        act = _ffnin(h2, w_ffn_in[l].astype(BF16), ffn_conv_w[l], ffn_conv_b[l].reshape(1, -1), seq)
        xt = _ffnout(act, w_ffn_out[l].astype(BF16), x1, mod3, final_norm_w.reshape(1, d), seq)
    return xt.reshape(batch, seq, d)
```

```python
import functools

import jax
import jax.numpy as jnp
from jax import lax
from jax.experimental import pallas as pl
from jax.experimental.pallas import tpu as pltpu

F32 = jnp.float32
BF16 = jnp.bfloat16

D_MODEL = 1024
SB_HEADS = 8
SB_HEAD_DIM = 64
SB_WIDTH = SB_HEADS * SB_HEAD_DIM
DN_HEADS = 8
DN_KEY_DIM = 64
DN_VAL_DIM = 128
DN_QK_WIDTH = DN_HEADS * DN_KEY_DIM
DN_V_WIDTH = DN_HEADS * DN_VAL_DIM
DN_CONV_CH = 2 * DN_QK_WIDTH + DN_V_WIDTH
DN_CONV_WIDTH = 4
DN_CHUNK = 64
D_FF = 2816
FFN_CONV_WIDTH = 3
NORM_EPS = 1e-6
L2_EPS = 1e-6

LANES = 128
HALO = 16

COL_DN = 0
COL_Z = DN_CONV_CH
COL_GATE = COL_Z + DN_V_WIDTH
COL_SB = COL_GATE + 2 * D_MODEL
PROJ_WIDTH = COL_SB + 3 * SB_WIDTH

VMEM_LIMIT = 56 * 1024 * 1024


def _dot(a, b):
    return jnp.dot(a, b, preferred_element_type=F32)


def _dot_nt(a, b):
    return lax.dot_general(a, b, (((1,), (1,)), ((), ())), preferred_element_type=F32)


def _dot_tn(a, b):
    return lax.dot_general(a, b, (((0,), (0,)), ((), ())), preferred_element_type=F32)


def _split2(a):
    hi = a.astype(BF16)
    lo = (a - hi.astype(F32)).astype(BF16)
    return hi, lo


def _split3(a):
    a1 = a.astype(BF16)
    r = a - a1.astype(F32)
    a2 = r.astype(BF16)
    a3 = (r - a2.astype(F32)).astype(BF16)
    return a1, a2, a3


def _dot_exact_lhs(m, b):
    b1, b2, b3 = _split3(b)
    return _dot(m, b1) + (_dot(m, b2) + _dot(m, b3))


def _dot3(a, b):
    a1, a2 = _split2(a)
    b1, b2 = _split2(b)
    return _dot(a1, b1) + (_dot(a1, b2) + _dot(a2, b1))


def _sigmoid(x):
    return 1.0 / (1.0 + jnp.exp(-x))


def _softplus(x):
    return jnp.maximum(x, 0.0) + jnp.log(1.0 + jnp.exp(-jnp.abs(x)))


def _ada_kernel(c_ref, w_ref, b_ref, o_ref):
    c = c_ref[...]
    ca = c * _sigmoid(c)
    o_ref[...] = _dot3(ca, w_ref[...]) + b_ref[...]


def _ada(c_pad, w_ada, b_ada):
    n = w_ada.shape[1]
    tn = 1536
    return pl.pallas_call(
        _ada_kernel,
        grid=(n // tn,),
        in_specs=[
            pl.BlockSpec((8, D_MODEL), lambda j: (0, 0)),
            pl.BlockSpec((D_MODEL, tn), lambda j: (0, j)),
            pl.BlockSpec((1, tn), lambda j: (0, j)),
        ],
        out_specs=pl.BlockSpec((8, tn), lambda j: (0, j)),
        out_shape=jax.ShapeDtypeStruct((8, n), F32),
        compiler_params=pltpu.CompilerParams(
            dimension_semantics=("arbitrary",), vmem_limit_bytes=VMEM_LIMIT),
        name="ada",
    )(c_pad, w_ada, b_ada)


def _inproj_kernel(x_ref, mod_ref, nw_ref, w_ref, wt_ref, o_ref, ot_ref, h_ref):
    @pl.when(pl.program_id(1) == 0)
    def _():
        x = x_ref[...]
        ms = jnp.mean(x * x, axis=-1, keepdims=True)
        y = x * lax.rsqrt(ms + NORM_EPS) * nw_ref[...]
        m = mod_ref[0]
        h = (y * (1.0 + m[1:2, :]) + m[0:1, :]).astype(BF16)
        h_ref[...] = h
        ot_ref[...] = _dot(h, wt_ref[...])

    o_ref[...] = _dot(h_ref[...], w_ref[...]).astype(o_ref.dtype)


def _inproj(x2, mod3, norm_w, w_main, w_tail, seq):
    t = x2.shape[0]
    tm, tn = 1024, PROJ_WIDTH // 2
    tiles_per_seq = seq // tm
    return pl.pallas_call(
        _inproj_kernel,
        grid=(t // tm, PROJ_WIDTH // tn),
        in_specs=[
            pl.BlockSpec((tm, D_MODEL), lambda i, j: (i, 0)),
            pl.BlockSpec((1, 6, D_MODEL), lambda i, j: (i // tiles_per_seq, 0, 0)),
            pl.BlockSpec((1, D_MODEL), lambda i, j: (0, 0)),
            pl.BlockSpec((D_MODEL, tn), lambda i, j: (0, j)),
            pl.BlockSpec((D_MODEL, LANES), lambda i, j: (0, 0)),
        ],
        out_specs=[
            pl.BlockSpec((tm, tn), lambda i, j: (i, j)),
            pl.BlockSpec((tm, LANES), lambda i, j: (i, 0)),
        ],
        out_shape=[
            jax.ShapeDtypeStruct((t, PROJ_WIDTH), BF16),
            jax.ShapeDtypeStruct((t, LANES), F32),
        ],
        scratch_shapes=[pltpu.VMEM((tm, D_MODEL), BF16)],
        compiler_params=pltpu.CompilerParams(
            dimension_semantics=("parallel", "arbitrary"), vmem_limit_bytes=VMEM_LIMIT),
        name="inproj",
    )(x2, mod3, norm_w, w_main, w_tail)


SB_T = 256
SB_DEAD_LOG = -100.0
SB_NO_TILE = -1e30


SB_TILES = 4


def _sb_kernel(q_ref, k_ref, v_ref, upper_ref, o_ref):
    upper = upper_ref[...]
    for s in range(SB_TILES):
        _sb_tile(pl.program_id(2) * SB_TILES + s, q_ref[s * SB_T:(s + 1) * SB_T, :],
                 k_ref, v_ref, upper, o_ref, s * SB_T)


def _sb_tile(i, q, k_ref, v_ref, upper, o_ref, row0):
    lane = lax.broadcasted_iota(jnp.int32, (1, LANES), 1)
    first = lane < SB_HEAD_DIM
    qs = q * jnp.asarray(SB_HEAD_DIM ** -0.5, BF16)
    zero = jnp.zeros_like(qs)
    H = SB_T // 2
    qh = (jnp.where(first, qs, zero), jnp.where(first, zero, qs))
    q_chunks = [qh[e][s * H:(s + 1) * H] for e in range(2) for s in range(2)]
    rh =lax.broadcasted_iota(jnp.int32, (H, SB_T), 0)
    ch = lax.broadcasted_iota(jnp.int32, (H, SB_T), 1)
    mask_lo = (ch < rh)[:, :H]
    mask_hi = ch < rh + H

    def log_terms(qcs, kbs, ups, masks):
        z = [_dot_nt(qc, kb) for qc, kb in zip(qcs, kbs)]
        sp = [_softplus(x) for x in z]
        l1 = [-x if m is None else jnp.where(m, -x, 0.0) for x, m in zip(sp, masks)]
        cs = [_dot(x.astype(BF16), up) for x, up in zip(l1, ups)]
        logw = [(x - s) + y for x, s, y in zip(z, sp, cs)]
        rowsum = [jnp.sum(x, axis=-1, keepdims=True) for x in l1]
        return logw, rowsum

    def weighted_values(logw, carries, masks, vbs):
        a = [jnp.exp(x + cr) for x, cr in zip(logw, carries)]
        a = [x if m is None else jnp.where(m, x, 0.0) for x, m in zip(a, masks)]
        return [_dot(x.astype(BF16), vb) for x, vb in zip(a, vbs)]

    def tile(ref, j):
        return ref[pl.ds(pl.multiple_of(jnp.maximum(j, 0) * SB_T, SB_T), SB_T), :]

    k_d, v_d = tile(k_ref, i), tile(v_ref, i)
    halves = (0, 1, 0, 1)
    masks = [mask_hi if s else mask_lo for s in halves] + [None] * 8
    logw, rs = log_terms(q_chunks * 3,
                         [k_d if s else k_d[:H] for s in halves] + [tile(k_ref, i - 1)] * 4 + [tile(k_ref, i - 2)] * 4,
                         [upper if s else upper[:H, :H] for s in halves] + [upper] * 8, masks)
    carry_near = [jnp.where(i > 0, x, SB_NO_TILE) for x in rs[:4]]
    carry_far = [jnp.where(i > 1, x + y, SB_NO_TILE) for x, y in zip(rs[:4], rs[4:8])]
    ov = weighted_values(logw, [jnp.zeros((H, 1), F32)] * 4 + carry_near + carry_far, masks,
                         [v_d if s else v_d[:H] for s in halves] + [tile(v_ref, i - 1)] * 4 + [tile(v_ref, i - 2)] * 4)
    o = [x + y + w for x, y, w in zip(ov[:4], ov[4:8], ov[8:])]
    carry = [x + jnp.where(i > 0, y, 0.0) + jnp.where(i > 1, w, 0.0) for x, y, w in zip(rs[:4], rs[4:8], rs[8:])]

    def live(carries):
        m = jnp.maximum(jnp.maximum(carries[0], carries[1]), jnp.maximum(carries[2], carries[3]))
        return jnp.max(m) > SB_DEAD_LOG

    def cond(st):
        n, go = st[0], st[1]
        return jnp.logical_and(2 * n < i, go)

    def body(st):
        n, _, o, carry = st
        j_near = i - 1 - 2 * n
        has_far = j_near > 0
        kbs = [tile(k_ref, j_near)] * 4 + [tile(k_ref, j_near - 1)] * 4
        vbs = [tile(v_ref, j_near)] * 4 + [tile(v_ref, j_near - 1)] * 4
        logw, rs = log_terms(q_chunks * 2, kbs, [upper] * 8, [None] * 8)
        carry_far = [jnp.where(has_far, c + x, SB_NO_TILE) for c, x in zip(carry, rs[:4])]
        ov = weighted_values(logw, carry + carry_far, [None] * 8, vbs)
        o = [x + y + w for x, y, w in zip(o, ov[:4], ov[4:])]
        carry = [c + x + jnp.where(has_far, y, 0.0) for c, x, y in zip(carry, rs[:4], rs[4:])]
        return n + 1, live(carry), o, carry

    st = lax.while_loop(cond, body, (jnp.int32(1), live(carry), o, carry))
    o = st[2]
    o_ref[row0:row0 + H, :] = jnp.where(first, o[0], o[2]).astype(o_ref.dtype)
    o_ref[row0 + H:row0 + SB_T, :] = jnp.where(first, o[1], o[3]).astype(o_ref.dtype)


def _sb_attention(proj, batch, seq):
    t = proj.shape[0]
    rows = SB_TILES * SB_T
    nq = seq // rows
    pairs = SB_HEADS // 2
    qcol = COL_SB // LANES
    kcol = qcol + SB_WIDTH // LANES
    vcol = kcol + SB_WIDTH // LANES
    return pl.pallas_call(
        _sb_kernel,
        grid=(batch, pairs, nq),
        in_specs=[
            pl.BlockSpec((rows, LANES), lambda b, p, i: (b * nq + i, qcol + p)),
            pl.BlockSpec((seq, LANES), lambda b, p, i: (b, kcol + p)),
            pl.BlockSpec((seq, LANES), lambda b, p, i: (b, vcol + p)),
            pl.BlockSpec((SB_T, SB_T), lambda b, p, i: (0, 0)),
        ],
        out_specs=pl.BlockSpec((rows, LANES), lambda b, p, i: (b * nq + i, p)),
        out_shape=jax.ShapeDtypeStruct((t, SB_WIDTH), BF16),
        compiler_params=pltpu.CompilerParams(
            dimension_semantics=("parallel", "parallel", "arbitrary"),
            vmem_limit_bytes=VMEM_LIMIT),
        name="sb",
    )(proj, proj, proj, jnp.tril(jnp.ones((SB_T, SB_T), BF16), -1))


DNP_T = 512


def _dnprep_kernel(cur_ref, halo_ref, tail_ref, cw_ref, alog_ref, dtb_ref,
                   q_ref, k_ref, v_ref, bg_ref, xs_ref, *, tiles_per_seq):
    i = pl.program_id(0)
    cur = cur_ref[...].astype(F32)
    prev = halo_ref[...].astype(F32)
    prev = jnp.where(i % tiles_per_seq == 0, 0.0, prev)
    xs_ref[0:HALO, :] = prev
    xs_ref[HALO:, :] = cur
    w = cw_ref[...]
    y = w[DN_CONV_WIDTH - 1:DN_CONV_WIDTH, :] * cur
    for d in range(1, DN_CONV_WIDTH):
        y = y + w[DN_CONV_WIDTH - 1 - d:DN_CONV_WIDTH - d, :] * xs_ref[pl.ds(HALO - d, DNP_T), :]
    s = y * _sigmoid(y)

    r = lax.broadcasted_iota(jnp.int32, (DN_QK_WIDTH, DN_QK_WIDTH), 0) // DN_KEY_DIM
    c = lax.broadcasted_iota(jnp.int32, (DN_QK_WIDTH, DN_QK_WIDTH), 1) // DN_KEY_DIM
    same_head = jnp.where(r == c, 1.0, 0.0).astype(BF16)

    def l2n(t):
        ss = _dot((t * t).astype(BF16), same_head)
        return t * lax.rsqrt(ss + L2_EPS)

    q = s[:, 0:DN_QK_WIDTH]
    k = s[:, DN_QK_WIDTH:2 * DN_QK_WIDTH]
    q_ref[...] = (l2n(q) * (DN_KEY_DIM ** -0.5)).astype(q_ref.dtype)
    k_ref[...] = l2n(k).astype(k_ref.dtype)
    v_ref[...] = s[:, 2 * DN_QK_WIDTH:].astype(v_ref.dtype)

    tl = tail_ref[...]
    lane = lax.broadcasted_iota(jnp.int32, (1, LANES), 1)
    beta = _sigmoid(tl)
    g = -jnp.exp(alog_ref[...]) * _softplus(tl + dtb_ref[...])
    bg_ref[...] = jnp.where(lane < DN_HEADS, beta, g)


def _dnprep(proj, tail, conv_w, alog_pad, dtb_pad, seq):
    t = proj.shape[0]
    tiles_per_seq = seq // DNP_T
    halo_blocks = DNP_T // HALO
    return pl.pallas_call(
        functools.partial(_dnprep_kernel, tiles_per_seq=tiles_per_seq),
        grid=(t // DNP_T,),
        in_specs=[
            pl.BlockSpec((DNP_T, DN_CONV_CH), lambda i: (i, 0)),
            pl.BlockSpec((HALO, DN_CONV_CH), lambda i: (jnp.maximum(i * halo_blocks - 1, 0), 0)),
            pl.BlockSpec((DNP_T, LANES), lambda i: (i, 0)),
            pl.BlockSpec((DN_CONV_WIDTH, DN_CONV_CH), lambda i: (0, 0)),
            pl.BlockSpec((1, LANES), lambda i: (0, 0)),
            pl.BlockSpec((1, LANES), lambda i: (0, 0)),
        ],
        out_specs=[
            pl.BlockSpec((DNP_T, DN_QK_WIDTH), lambda i: (i, 0)),
            pl.BlockSpec((DNP_T, DN_QK_WIDTH), lambda i: (i, 0)),
            pl.BlockSpec((DNP_T, DN_V_WIDTH), lambda i: (i, 0)),
            pl.BlockSpec((DNP_T, LANES), lambda i: (i, 0)),
        ],
        out_shape=[
            jax.ShapeDtypeStruct((t, DN_QK_WIDTH), BF16),
            jax.ShapeDtypeStruct((t, DN_QK_WIDTH), BF16),
            jax.ShapeDtypeStruct((t, DN_V_WIDTH), BF16),
            jax.ShapeDtypeStruct((t, LANES), F32),
        ],
        scratch_shapes=[pltpu.VMEM((DNP_T + HALO, DN_CONV_CH), F32)],
        compiler_params=pltpu.CompilerParams(
            dimension_semantics=("parallel",), vmem_limit_bytes=VMEM_LIMIT),
        name="dnprep",
    )(proj, proj, tail, conv_w, alog_pad, dtb_pad)


DN_GROUP = 4
DN_GROUP_LANES = DN_GROUP * DN_KEY_DIM
DN_ROWS = DN_HEADS * DN_CHUNK
DNL_CHUNKS = 8


def _bd4(x, mask):
    zero = jnp.zeros((), x.dtype)
    return jnp.concatenate([jnp.where(m, x, zero) for m in mask], axis=0)


def _dot3_bd(a, b, mask):
    a1, a2 = _split2(a)
    b1, b2 = _split2(b)
    bd1 = _bd4(b1, mask)
    bd2 = _bd4(b2, mask)
    return _dot(a1, bd1) + (_dot(a1, bd2) + _dot(a2, bd1))


def _dot1_bd(a, b, mask):
    return _dot(a.astype(BF16), _bd4(b.astype(BF16), mask))


def _dnlocal_kernel(q_ref, k_ref, v_ref, bg_ref, u_ref, wqk_ref, intra_ref, eg_ref):
    C = DN_CHUNK
    GL = DN_GROUP_LANES
    ri = lax.broadcasted_iota(jnp.int32, (C, GL), 0)
    cj = lax.broadcasted_iota(jnp.int32, (C, GL), 1) % C
    incl = ri >= cj
    strict = ri > cj
    after = jnp.where(strict, 1.0, 0.0)
    eye = jnp.where(ri == cj, 1.0, 0.0)
    blk = {b: (ri // b) == (cj // b) for b in (8, 16, 32, 64)}
    r2 = lax.broadcasted_iota(jnp.int32, (C, C), 0)
    c2 = lax.broadcasted_iota(jnp.int32, (C, C), 1)
    lower_incl = jnp.where(r2 >= c2, 1.0, 0.0).astype(BF16)
    head_of_lane = lax.broadcasted_iota(jnp.int32, (1, GL), 1) // C
    bdmask = [head_of_lane == r for r in range(DN_GROUP)]
    lane = lax.broadcasted_iota(jnp.int32, (1, LANES), 1)
    first = lane < DN_KEY_DIM
    sels = (first, jnp.logical_not(first))

    chains = [(c, g) for c in range(DNL_CHUNKS) for g in range(DN_HEADS // DN_GROUP)]
    rows_of = lambda c: slice(c * C, (c + 1) * C)
    lanes_of = lambda g: slice(g * GL, (g + 1) * GL)
    each = lambda f, *ls: [f(*xs) for xs in zip(*ls)]

    def head_bcast(c, g, col0):
        bgc = bg_ref[rows_of(c), :]
        return [jnp.broadcast_to(bgc[:, col0 + h:col0 + h + 1], (C, LANES))
                for h in range(g * DN_GROUP, (g + 1) * DN_GROUP)]

    def cat(hs):
        return jnp.concatenate([jnp.where(first, hs[0], hs[1]), jnp.where(first, hs[2], hs[3])], axis=1)

    beta_h = [head_bcast(c, g, 0) for c, g in chains]
    beta_cat = each(cat, beta_h)
    g_cat = [cat(head_bcast(c, g, DN_HEADS)) for c, g in chains]
    k_cat = [k_ref[rows_of(c), lanes_of(g)] for c, g in chains]
    q_cat = [q_ref[rows_of(c), lanes_of(g)] for c, g in chains]

    gc = [_dot_exact_lhs(lower_incl, x) for x in g_cat]
    diff = [_dot_exact_lhs(lower_incl, x * after) for x in g_cat]
    kq = each(lambda k, q: _dot_nt(jnp.concatenate([k, q], axis=0), _bd4(k, bdmask)), k_cat, q_cat)
    decay = [jnp.where(incl, jnp.exp(x), 0.0) for x in diff]
    lmat = each(lambda x, b, d: jnp.where(strict, x[:C] * b * d, 0.0), kq, beta_cat, decay)
    for (c, g), x, d in zip(chains, kq, decay):
        intra_ref[rows_of(c), lanes_of(g)] = jnp.where(incl, x[C:] * d, 0.0).astype(intra_ref.dtype)

    pw = [jnp.where(blk[8], -x, 0.0) for x in lmat]
    tinv = [eye + x for x in pw]
    p2 = [_dot3_bd(x, x, bdmask) for x in pw]
    both = each(lambda t, p: _dot3_bd(jnp.concatenate([t, p], axis=0), p, bdmask), tinv, p2)
    tinv = each(lambda t, x: t + x[:C], tinv, both)
    tinv = each(lambda t, x: t + _dot3_bd(t, x[C:], bdmask), tinv, both)
    for b in (8, 16, 32):
        offd = jnp.logical_and(blk[2 * b], jnp.logical_not(blk[b]))
        y = each(lambda l, t: _dot1_bd(jnp.where(offd, l, 0.0), t, bdmask), lmat, tinv)
        tinv = each(lambda t, x: t - _dot1_bd(t, x, bdmask), tinv, y)
    t_bd = [_bd4(t.astype(BF16), bdmask) for t in tinv]

    rhs = []
    qk_out = []
    for (c, g), gcx, k, q, bcat, bh in zip(chains, gc, k_cat, q_cat, beta_cat, beta_h):
        egc = jnp.exp(gcx)
        g_last = gcx[C - 1:C, :]
        kf = k.astype(F32)
        kbe = kf * bcat * egc
        qg = q.astype(F32) * egc
        kdec = kf * jnp.exp(g_last - gcx)
        eg_ref[c, :, lanes_of(g)] = jnp.exp(g_last)
        rhs_rows = []
        qk_rows = []
        for hh in range(DN_GROUP):
            h = g * DN_GROUP + hh
            slab = slice((hh // 2) * LANES, (hh // 2 + 1) * LANES)
            sel = sels[hh % 2]
            vb = v_ref[rows_of(c), h * DN_VAL_DIM:(h + 1) * DN_VAL_DIM].astype(F32) * bh[hh]
            rhs_rows.append(jnp.concatenate([vb, jnp.where(sel, kbe[:, slab], 0.0)], axis=1))
            qk_rows.append(jnp.concatenate([jnp.where(sel, qg[:, slab], 0.0),
                                            jnp.where(sel, kdec[:, slab], 0.0)], axis=1))
        rhs.append(jnp.concatenate(rhs_rows, axis=0).astype(BF16))
        qk_out.append(jnp.concatenate(qk_rows, axis=0))
    uw = each(_dot, t_bd, rhs)
    for (c, g), x, qkx in zip(chains, uw, qk_out):
        rows_g = slice(g * DN_GROUP * C, (g + 1) * DN_GROUP * C)
        u_ref[c, rows_g, :] = x[:, :DN_VAL_DIM]
        wqk_ref[c, rows_g, 0:LANES] = x[:, DN_VAL_DIM:].astype(wqk_ref.dtype)
        wqk_ref[c, rows_g, LANES:] = qkx.astype(wqk_ref.dtype)


def _dnlocal(qn, kn, v, bg, batch, seq):
    ntot = batch * seq // DN_CHUNK
    nsteps = ntot // DNL_CHUNKS
    rows = DNL_CHUNKS * DN_CHUNK
    row = lambda i: (i, 0)
    blk3 = lambda i: (i, 0, 0)
    return pl.pallas_call(
        _dnlocal_kernel,
        grid=(nsteps,),
        in_specs=[
            pl.BlockSpec((rows, DN_QK_WIDTH), row),
            pl.BlockSpec((rows, DN_QK_WIDTH), row),
            pl.BlockSpec((rows, DN_V_WIDTH), row),
            pl.BlockSpec((rows, LANES), row),
        ],
        out_specs=[
            pl.BlockSpec((DNL_CHUNKS, DN_ROWS, DN_VAL_DIM), blk3),
            pl.BlockSpec((DNL_CHUNKS, DN_ROWS, 3 * LANES), blk3),
            pl.BlockSpec((rows, DN_QK_WIDTH), row),
            pl.BlockSpec((DNL_CHUNKS, 1, DN_QK_WIDTH), blk3),
        ],
        out_shape=[
            jax.ShapeDtypeStruct((ntot, DN_ROWS, DN_VAL_DIM), F32),
            jax.ShapeDtypeStruct((ntot, DN_ROWS, 3 * LANES), BF16),
            jax.ShapeDtypeStruct((ntot * DN_CHUNK, DN_QK_WIDTH), BF16),
            jax.ShapeDtypeStruct((ntot, 1, DN_QK_WIDTH), F32),
        ],
        compiler_params=pltpu.CompilerParams(
            dimension_semantics=("parallel",), vmem_limit_bytes=VMEM_LIMIT),
        name="dnlocal",
    )(qn, kn, v, bg)


def _dnscan_kernel(eg_ref, u_ref, wqk_ref, intra_ref, z_ref, nw_ref, o_ref, state_ref, *,
                   batch, nchunks):
    C = DN_CHUNK
    n = pl.program_id(0)

    @pl.when(n == 0)
    def _():
        state_ref[...] = jnp.zeros_like(state_ref)

    rb = lax.broadcasted_iota(jnp.int32, (2 * C, LANES), 0) // C
    cb = lax.broadcasted_iota(jnp.int32, (2 * C, LANES), 1) // C
    bd2 = rb == cb
    nw = nw_ref[...]

    chains = [(b, p) for b in range(batch) for p in range(DN_HEADS // 2)]
    rows = lambda p: slice(2 * p * C, (2 * p + 2) * C)
    sidx = lambda b, p, e: b * DN_HEADS + 2 * p + e
    s_old = [(state_ref[sidx(b, p, 0)], state_ref[sidx(b, p, 1)]) for b, p in chains]
    s_bf = [jnp.concatenate(s, axis=0).astype(BF16) for s in s_old]
    wq = [jnp.concatenate([wqk_ref[b, 0, rows(p), 0:LANES], wqk_ref[b, 0, rows(p), LANES:2 * LANES]], axis=0)
          for b, p in chains]
    ws_qs = [_dot(a, s) for a, s in zip(wq, s_bf)]
    v_new = [(u_ref[b, 0, rows(p), :] - x[:2 * C]).astype(BF16) for (b, p), x in zip(chains, ws_qs)]
    ibd = []
    for b, p in chains:
        islab = intra_ref[b, :, p * LANES:(p + 1) * LANES]
        ibd.append(jnp.where(bd2, jnp.concatenate([islab, islab], axis=0), jnp.zeros((), islab.dtype)))
    o = [x[2 * C:] + _dot(a, v) for x, a, v in zip(ws_qs, ibd, v_new)]
    kv = [_dot_tn(wqk_ref[b, 0, rows(p), 2 * LANES:], v) for (b, p), v in zip(chains, v_new)]
    for (b, p), s, x in zip(chains, s_old, kv):
        base = (b * nchunks + n) * DN_HEADS + 2 * p
        state_ref[sidx(b, p, 0)] = s[0] * eg_ref[base] + x[:C]
        state_ref[sidx(b, p, 1)] = s[1] * eg_ref[base + 1] + x[C:]
    for (b, p), x in zip(chains, o):
        for e in range(2):
            h = 2 * p + e
            oh = x[e * C:(e + 1) * C]
            ms = jnp.mean(oh * oh, axis=-1, keepdims=True)
            zh = z_ref[b, :, h * DN_VAL_DIM:(h + 1) * DN_VAL_DIM].astype(F32)
            on = oh * lax.rsqrt(ms + NORM_EPS) * nw * (zh * _sigmoid(zh))
            o_ref[b, :, h * DN_VAL_DIM:(h + 1) * DN_VAL_DIM] = on.astype(o_ref.dtype)


def _dnscan(eg_flat, u, wqk, intra, proj, dn_norm_w, batch, seq):
    nchunks = seq // DN_CHUNK
    zcol = COL_Z // DN_V_WIDTH
    blk4 = lambda n: (0, n, 0, 0)
    out = pl.pallas_call(
        functools.partial(_dnscan_kernel, batch=batch, nchunks=nchunks),
        grid=(nchunks,),
        in_specs=[
            pl.BlockSpec(memory_space=pltpu.SMEM),
            pl.BlockSpec((batch, 1, DN_ROWS, DN_VAL_DIM), blk4),
            pl.BlockSpec((batch, 1, DN_ROWS, 3 * LANES), blk4),
            pl.BlockSpec((batch, DN_CHUNK, DN_QK_WIDTH), lambda n: (0, n, 0)),
            pl.BlockSpec((batch, DN_CHUNK, DN_V_WIDTH), lambda n: (0, n, zcol)),
            pl.BlockSpec((1, DN_VAL_DIM), lambda n: (0, 0)),
        ],
        out_specs=pl.BlockSpec((batch, DN_CHUNK, DN_V_WIDTH), lambda n: (0, n, 0)),
        out_shape=jax.ShapeDtypeStruct((batch, seq, DN_V_WIDTH), BF16),
        scratch_shapes=[pltpu.VMEM((batch * DN_HEADS, DN_KEY_DIM, DN_VAL_DIM), F32)],
        compiler_params=pltpu.CompilerParams(
            dimension_semantics=("arbitrary",), vmem_limit_bytes=VMEM_LIMIT),
        name="dnscan",
    )(eg_flat,
      u.reshape(batch, nchunks, DN_ROWS, DN_VAL_DIM),
      wqk.reshape(batch, nchunks, DN_ROWS, 3 * LANES),
      intra.reshape(batch, seq, DN_QK_WIDTH),
      proj.reshape(batch, seq, PROJ_WIDTH),
      dn_norm_w)
    return out.reshape(batch * seq, DN_V_WIDTH)


MG_T = 512


def _merge_kernel(oa_ref, ob_ref, ga_ref, gb_ref, x_ref, mod_ref, wsb_ref, wdn_ref, wout_ref,
                  nw_ref, x1_ref, h2_ref):
    pa = _dot(oa_ref[...], wsb_ref[...])
    pb = _dot(ob_ref[...], wdn_ref[...])
    merged = _sigmoid(ga_ref[...].astype(F32)) * pa + _sigmoid(gb_ref[...].astype(F32)) * pb
    y = _dot(merged.astype(BF16), wout_ref[...])
    m = mod_ref[0]
    x1 = x_ref[...] + m[2:3, :] * y
    x1_ref[...] = x1
    ms = jnp.mean(x1 * x1, axis=-1, keepdims=True)
    hn = x1 * lax.rsqrt(ms + NORM_EPS) * nw_ref[...]
    h2_ref[...] = (hn * (1.0 + m[4:5, :]) + m[3:4, :]).astype(h2_ref.dtype)


def _merge(o_a, o_b, proj, x2, mod3, w_sb, w_dn, w_out, norm2_w, seq):
    t = x2.shape[0]
    tiles_per_seq = seq // MG_T
    gcol = COL_GATE // D_MODEL
    const = lambda i: (0, 0)
    return pl.pallas_call(
        _merge_kernel,
        grid=(t // MG_T,),
        in_specs=[
            pl.BlockSpec((MG_T, SB_WIDTH), lambda i: (i, 0)),
            pl.BlockSpec((MG_T, DN_V_WIDTH), lambda i: (i, 0)),
            pl.BlockSpec((MG_T, D_MODEL), lambda i: (i, gcol)),
            pl.BlockSpec((MG_T, D_MODEL), lambda i: (i, gcol + 1)),
            pl.BlockSpec((MG_T, D_MODEL), lambda i: (i, 0)),
            pl.BlockSpec((1, 6, D_MODEL), lambda i: (i // tiles_per_seq, 0, 0)),
            pl.BlockSpec((SB_WIDTH, D_MODEL), const),
            pl.BlockSpec((DN_V_WIDTH, D_MODEL), const),
            pl.BlockSpec((D_MODEL, D_MODEL), const),
            pl.BlockSpec((1, D_MODEL), const),
        ],
        out_specs=[
            pl.BlockSpec((MG_T, D_MODEL), lambda i: (i, 0)),
            pl.BlockSpec((MG_T, D_MODEL), lambda i: (i, 0)),
        ],
        out_shape=[
            jax.ShapeDtypeStruct((t, D_MODEL), F32),
            jax.ShapeDtypeStruct((t, D_MODEL), BF16),
        ],
        compiler_params=pltpu.CompilerParams(
            dimension_semantics=("parallel",), vmem_limit_bytes=VMEM_LIMIT),
        name="merge",
    )(o_a, o_b, proj, proj, x2, mod3, w_sb, w_dn, w_out, norm2_w)


FI_TM = 512
FI_TN = 256


def _ffnin_kernel(h_ref, halo_ref, w_ref, cw_ref, b_ref, o_ref, s_ref, *, tiles_per_seq):
    i = pl.program_id(0)
    h = h_ref[...]
    hp = halo_ref[...]
    keep = jnp.where(i % tiles_per_seq == 0, 0.0, 1.0)
    nsub = D_FF // FI_TN

    def cols(br, j):
        return slice(br * D_FF + j * FI_TN, br * D_FF + (j + 1) * FI_TN)

    def matmuls(j):
        for br in range(2):
            w = w_ref[:, cols(br, j)]
            s_ref[2 * (j % 2) + br, 0:HALO, :] = _dot(hp, w) * keep
            s_ref[2 * (j % 2) + br, HALO:, :] = _dot(h, w)

    def conv_act(j):
        ys = []
        for br in range(2):
            slot = 2 * (j % 2) + br
            cw = cw_ref[:, cols(br, j)]
            y = b_ref[:, cols(br, j)]
            for d in range(FFN_CONV_WIDTH):
                y = y + cw[FFN_CONV_WIDTH - 1 - d:FFN_CONV_WIDTH - d, :] * s_ref[slot, pl.ds(HALO - d, FI_TM), :]
            ys.append(y)
        o_ref[:, j * FI_TN:(j + 1) * FI_TN] = (ys[0] * _sigmoid(ys[0]) * ys[1]).astype(o_ref.dtype)

    matmuls(0)
    for j in range(nsub):
        if j + 1 < nsub:
            matmuls(j + 1)
        conv_act(j)


def _ffnin(h2, w_ffn_in, conv_w, conv_b, seq):
    t = h2.shape[0]
    tiles_per_seq = seq // FI_TM
    halo_blocks = FI_TM // HALO
    const = lambda i: (0, 0)
    return pl.pallas_call(
        functools.partial(_ffnin_kernel, tiles_per_seq=tiles_per_seq),
        grid=(t // FI_TM,),
        in_specs=[
            pl.BlockSpec((FI_TM, D_MODEL), lambda i: (i, 0)),
            pl.BlockSpec((HALO, D_MODEL), lambda i: (jnp.maximum(i * halo_blocks - 1, 0), 0)),
            pl.BlockSpec((D_MODEL, 2 * D_FF), const),
            pl.BlockSpec((FFN_CONV_WIDTH, 2 * D_FF), const),
            pl.BlockSpec((1, 2 * D_FF), const),
        ],
        out_specs=pl.BlockSpec((FI_TM, D_FF), lambda i: (i, 0)),
        out_shape=jax.ShapeDtypeStruct((t, D_FF), BF16),
        scratch_shapes=[pltpu.VMEM((4, FI_TM + HALO, FI_TN), F32)],
        compiler_params=pltpu.CompilerParams(
            dimension_semantics=("parallel",), vmem_limit_bytes=VMEM_LIMIT),
        name="ffnin",
    )(h2, h2, w_ffn_in, conv_w, conv_b)


FO_T = 1024


def _ffnout_kernel(a_ref, w_ref, x1_ref, mod_ref, nw_ref, o_ref):
    y = _dot(a_ref[...], w_ref[...])
    m = mod_ref[0]
    x2 = x1_ref[...] + m[5:6, :] * y
    ms = jnp.mean(x2 * x2, axis=-1, keepdims=True)
    o_ref[...] = x2 * lax.rsqrt(ms + NORM_EPS) * nw_ref[...]


def _ffnout(act, w_ffn_out, x1, mod3, final_w, seq):
    t = x1.shape[0]
    tiles_per_seq = seq // FO_T
    return pl.pallas_call(
        _ffnout_kernel,
        grid=(t // FO_T,),
        in_specs=[
            pl.BlockSpec((FO_T, D_FF), lambda i: (i, 0)),
            pl.BlockSpec((D_FF, D_MODEL), lambda i: (0, 0)),
            pl.BlockSpec((FO_T, D_MODEL), lambda i: (i, 0)),
            pl.BlockSpec((1, 6, D_MODEL), lambda i: (i // tiles_per_seq, 0, 0)),
            pl.BlockSpec((1, D_MODEL), lambda i: (0, 0)),
        ],
        out_specs=pl.BlockSpec((FO_T, D_MODEL), lambda i: (i, 0)),
        out_shape=jax.ShapeDtypeStruct((t, D_MODEL), F32),
        compiler_params=pltpu.CompilerParams(
            dimension_semantics=("parallel",), vmem_limit_bytes=VMEM_LIMIT),
        name="ffnout",
    )(act, w_ffn_out, x1, mod3, final_w)


def kernel(x, c, w_ada, b_ada, norm1_w, w_in, dn_conv_w, dn_A_log, dn_dt_bias, dn_norm_w,
           w_proj_sb, w_proj_dn, w_out, norm2_w, w_ffn_in, ffn_conv_w, ffn_conv_b, w_ffn_out,
           final_norm_w):
    batch, seq, d = x.shape
    depth = w_ada.shape[0]
    assert depth == 1, "the final rmsnorm is fused into the last layer's FFN-out kernel"
    t = batch * seq
    xt = x.reshape(t, d)
    c_pad = jnp.pad(c, ((0, 8 - batch), (0, 0)))

    sb_end = 3 * SB_WIDTH
    dn_end = sb_end + DN_CONV_CH
    z_end = dn_end + DN_V_WIDTH
    tail_end = z_end + 2 * DN_HEADS

    for l in range(depth):
        wl = w_in[l]
        w_main = jnp.concatenate(
            [wl[:, sb_end:z_end].astype(BF16), wl[:, tail_end:].astype(BF16), wl[:, :sb_end].astype(BF16)],
            axis=1)
        w_tail = jnp.pad(wl[:, z_end:tail_end], ((0, 0), (0, LANES - 2 * DN_HEADS))).astype(BF16)
        alog_pad = jnp.pad(dn_A_log[l], (DN_HEADS, LANES - 2 * DN_HEADS)).reshape(1, LANES)
        dtb_pad = jnp.pad(dn_dt_bias[l], (DN_HEADS, LANES - 2 * DN_HEADS)).reshape(1, LANES)

        mod = _ada(c_pad, w_ada[l], b_ada[l].reshape(1, -1))
        mod3 = mod[:batch].reshape(batch, 6, d)

        proj, tail = _inproj(xt, mod3, norm1_w[l].reshape(1, d), w_main, w_tail, seq)
        o_a = _sb_attention(proj, batch, seq)
        qn, kn, vv, bg = _dnprep(proj, tail, dn_conv_w[l], alog_pad, dtb_pad, seq)
        u, wqk, intra, eg = _dnlocal(qn, kn, vv, bg, batch, seq)
        eg_flat = eg[:, 0, ::DN_KEY_DIM].reshape(-1)
        o_b = _dnscan(eg_flat, u, wqk, intra, proj, dn_norm_w[l].reshape(1, -1), batch, seq)
        x1, h2 = _merge(o_a, o_b, proj, xt, mod3, w_proj_sb[l].astype(BF16),
                        w_proj_dn[l].astype(BF16), w_out[l].astype(BF16),
                        norm2_w[l].reshape(1, d), seq)
        act = _ffnin(h2, w_ffn_in[l].astype(BF16), ffn_conv_w[l], ffn_conv_b[l].reshape(1, -1), seq)
        xt = _ffnout(act, w_ffn_out[l].astype(BF16), x1, mod3, final_norm_w.reshape(1, d), seq)
    return xt.reshape(batch, seq, d)
```
